```python
import jax
import jax.numpy as jnp
from jax import lax
import numpy as np

D_MODEL = 1024
BATCH = 8
SEQ = 4096
DEPTH = 1

N_MEM = 256
EPS = 1e-6
GLA_HEADS = 4
GLA_DK = D_MODEL // 8
GLA_DV = D_MODEL // 4
GLA_QK = GLA_HEADS * GLA_DK
GLA_V = GLA_HEADS * GLA_DV
GLA_GATE_RANK = 16
GLA_TAU = 16.0
GLA_CHUNK = 64
DIL_GROUPS = ((128, 1), (512, 4), (2048, 16))
N_GROUPS = len(DIL_GROUPS)
DIL_HEADS = 8
HEAD_DIM = 64
DIL_W = N_GROUPS * DIL_HEADS * HEAD_DIM
DIL_OUT = DIL_HEADS * HEAD_DIM
DIL_BLOCK = 128
ROT_DIM = HEAD_DIM // 4
ROPE_THETA = 500000.0
X_HEADS = 4
X_HEAD_DIM = D_MODEL // X_HEADS
D_FF = 2816
CONV_W = 3
IN_SIZES = (GLA_QK, GLA_QK, GLA_V, GLA_V, GLA_GATE_RANK, DIL_W, DIL_W, DIL_W)
IN_COLS = sum(IN_SIZES)
IN_OFFSETS = tuple(sum(IN_SIZES[:i + 1]) for i in range(len(IN_SIZES) - 1))

kernel_name = 'hybrid_gla_dilated_memxattn_convglu'


def rms_norm(t, g):
    tf = t.astype(jnp.float32)
    y = tf * lax.rsqrt(jnp.mean(tf * tf, axis=-1, keepdims=True) + EPS)
    return (y * g.astype(jnp.float32)).astype(t.dtype)


def rope_partial(t, cos, sin):
    shp = cos.shape[:2] + (1,) * (t.ndim - 3) + cos.shape[2:]
    c, s_ = cos.reshape(shp), sin.reshape(shp)
    tf = t.astype(jnp.float32)
    x1, x2 = tf[..., :ROT_DIM // 2], tf[..., ROT_DIM // 2:ROT_DIM]
    return jnp.concatenate([x1 * c - x2 * s_, x2 * c + x1 * s_, tf[..., ROT_DIM:]], axis=-1)


def gla_chunked(q, k, v, log_a):
    b, h, s, dk = q.shape
    dv = v.shape[-1]
    n = s // GLA_CHUNK

    def chunks(t):
        return t.reshape(b, h, n, GLA_CHUNK, t.shape[-1]).transpose(2, 0, 1, 3, 4)

    causal = jnp.tril(jnp.ones((GLA_CHUNK, GLA_CHUNK), dtype=bool))[:, :, None]

    def step(state, inp):
        qc, kc, vc, gc = inp
        cum = jnp.cumsum(gc, axis=-2)
        o_inter = jnp.einsum('bhik,bhkv->bhiv', qc * jnp.exp(cum), state)
        rel = cum[:, :, :, None, :] - cum[:, :, None, :, :]
        decay = jnp.exp(jnp.where(causal, rel, -jnp.inf))
        attn = jnp.einsum('bhik,bhjk,bhijk->bhij', qc, kc, decay)
        o_intra = jnp.einsum('bhij,bhjv->bhiv', attn, vc)
        last = cum[:, :, -1, :]
        state = state * jnp.exp(last)[..., None] + jnp.einsum(
            'bhjk,bhjv->bhkv', kc * jnp.exp(last[:, :, None, :] - cum), vc)
        return state, o_inter + o_intra

    state0 = jnp.zeros((b, h, dk, dv), jnp.float32)
    _, o = lax.scan(step, state0, (chunks(q), chunks(k), chunks(v), chunks(log_a)))
    return o.transpose(1, 2, 0, 3, 4).reshape(b, h, s, dv)


def banded_attention(q, k, v, n_back):
    lead = q.shape[:-2]
    L, hd = q.shape[-2:]
    blk = DIL_BLOCK
    nb = -(-L // blk)
    lp = nb * blk
    pad = [(0, 0)] * len(lead) + [(0, lp - L), (0, 0)]
    q, k, v = (jnp.pad(t, pad) for t in (q, k, v))
    qb = q.reshape(*lead, nb, blk, hd)

    def kv_blocks(t):
        cur = t.reshape(*lead, nb, blk, hd)
        prev = jnp.pad(t, [(0, 0)] * len(lead) + [(blk, 0), (0, 0)])[..., :lp, :].reshape(*lead, nb, blk, hd)
        return jnp.concatenate([prev, cur], axis=-2)

    kb, vb = kv_blocks(k), kv_blocks(v)
    s = jnp.einsum('...nqd,...nkd->...nqk', qb, kb) * (hd ** -0.5)
    kj = jnp.arange(2 * blk)[None, :]
    dist = (jnp.arange(blk)[:, None] + blk) - kj
    band = (dist >= 0) & (dist <= n_back)
    kabs = jnp.arange(nb)[:, None] * blk - blk + kj
    mask = band[None, :, :] & (kabs >= 0)[:, None, :]
    s = jnp.where(mask, s, -jnp.inf)
    m = jnp.max(s, axis=-1, keepdims=True)
    p = jnp.exp(s - m)
    l = jnp.sum(p, axis=-1, keepdims=True)
    o = jnp.einsum('...nqk,...nkd->...nqd', p, vb) / l
    lse = (m + jnp.log(l))[..., 0]
    return o.reshape(*lead, lp, hd)[..., :L, :], lse.reshape(*lead, lp)[..., :L]


def dilated_attention(q, k, v):
    b, s = q.shape[:2]
    outs, lses = [], []
    for g, (win, dil) in enumerate(DIL_GROUPS):
        L = s // dil

        def sub(t):
            return t[:, :, g].reshape(b, L, dil, DIL_HEADS, HEAD_DIM).transpose(0, 2, 3, 1, 4)

        o, lse = banded_attention(sub(q), sub(k), sub(v), win // dil)
        outs.append(o.transpose(0, 3, 1, 2, 4).reshape(b, s, DIL_HEADS, HEAD_DIM))
        lses.append(lse.transpose(0, 3, 1, 2).reshape(b, s, DIL_HEADS))
    w = jax.nn.softmax(jnp.stack(lses, axis=0), axis=0)
    return jnp.einsum('gbsh,gbshd->bshd', w, jnp.stack(outs, axis=0))


def mixer_sublayer(h, cos, sin, norm_mix, w_in, w_gla_gate, b_gla_gate, gla_out_norm, dil_q_norm, dil_k_norm,
                   w_br_gla, w_br_dil, w_merge_gate, b_merge_gate, w_mix_out):
    b, s, _ = h.shape
    dt = h.dtype
    xn = rms_norm(h, norm_mix)
    qa, ka, va, ra, za, qb, kb, vb = jnp.split(xn @ w_in, IN_OFFSETS, axis=-1)

    def heads(t):
        return t.reshape(b, s, GLA_HEADS, -1).transpose(0, 2, 1, 3).astype(jnp.float32)

    log_a = jax.nn.log_sigmoid((za @ w_gla_gate + b_gla_gate).astype(jnp.float32)) / GLA_TAU
    o_a = gla_chunked(heads(qa) * (GLA_DK ** -0.5), heads(ka), heads(va), heads(log_a))
    o_a = rms_norm(o_a, gla_out_norm).transpose(0, 2, 1, 3).reshape(b, s, GLA_V).astype(dt) * jax.nn.silu(ra)

    def dheads(t):
        return t.reshape(b, s, N_GROUPS, DIL_HEADS, HEAD_DIM)

    qd = rope_partial(rms_norm(dheads(qb), dil_q_norm), cos, sin)
    kd = rope_partial(rms_norm(dheads(kb), dil_k_norm), cos, sin)
    vd = dheads(vb).astype(jnp.float32)
    o_b = dilated_attention(qd, kd, vd).reshape(b, s, DIL_OUT).astype(dt)

    gates = jax.nn.sigmoid((xn @ w_merge_gate + b_merge_gate).astype(jnp.float32)).astype(dt)
    g_a, g_b = jnp.split(gates, 2, axis=-1)
    merged = g_a * (o_a @ w_br_gla) + g_b * (o_b @ w_br_dil)
    return h + merged @ w_mix_out


def memory_cross_attention(h, mem, norm_x, norm_mem, w_xq, w_xkv, x_q_norm, x_k_norm, w_xo):
    b, s, _ = h.shape
    m = mem.shape[1]
    xn = rms_norm(h, norm_x)
    mn = rms_norm(mem, norm_mem)
    q = rms_norm((xn @ w_xq).reshape(b, s, X_HEADS, X_HEAD_DIM), x_q_norm).astype(jnp.float32)
    k, v = jnp.split(mn @ w_xkv, 2, axis=-1)
    k = rms_norm(k.reshape(b, m, X_HEADS, X_HEAD_DIM), x_k_norm).astype(jnp.float32)
    v = v.reshape(b, m, X_HEADS, X_HEAD_DIM).astype(jnp.float32)
    p = jax.nn.softmax(jnp.einsum('bshd,bmhd->bhsm', q, k) * (X_HEAD_DIM ** -0.5), axis=-1)
    o = jnp.einsum('bhsm,bmhd->bshd', p, v).reshape(b, s, D_MODEL).astype(h.dtype)
    return h + o @ w_xo


def conv_glu_ffn(h, norm_ffn, w_ffn_up, w_ffn_conv, b_ffn_conv, w_ffn_down):
    s = h.shape[1]
    xn = rms_norm(h, norm_ffn)
    a, u = jnp.split(xn @ w_ffn_up, 2, axis=-1)
    ap = jnp.pad(a, ((0, 0), (CONV_W - 1, 0), (0, 0)))
    c = b_ffn_conv
    for i in range(CONV_W):
        c = c + ap[:, i:i + s, :] * w_ffn_conv[i]
    y = jax.nn.gelu(c, approximate=False) * u
    return h + y @ w_ffn_down


def setup_inputs(seed: int = 0) -> dict:
    key = jax.random.key(seed)
    ks = iter(jax.random.split(key, 32))
    f32 = jnp.float32

    def nrm(shape, fan_in):
        return jax.random.normal(next(ks), shape, f32) * (fan_in ** -0.5)

    def gain(shape):
        return 1.0 + 0.02 * jax.random.normal(next(ks), shape, f32)

    def bias(shape):
        return 0.01 * jax.random.normal(next(ks), shape, f32)

    x = jax.random.normal(next(ks), (BATCH, SEQ, D_MODEL), f32)
    mem = jax.random.normal(next(ks), (BATCH, N_MEM, D_MODEL), f32)
    starts = jax.random.randint(next(ks), (BATCH, 1), 0, 1024, dtype=jnp.int32)
    positions = starts + jnp.arange(SEQ, dtype=jnp.int32)[None, :]
    L = DEPTH
    return {
        'x': x, 'mem': mem, 'positions': positions,
        'norm_mix': gain((L, D_MODEL)),
        'w_in': nrm((L, D_MODEL, IN_COLS), D_MODEL),
        'w_gla_gate': nrm((L, GLA_GATE_RANK, GLA_QK), GLA_GATE_RANK),
        'b_gla_gate': bias((L, GLA_QK)),
        'gla_out_norm': gain((L, GLA_DV)),
        'dil_q_norm': gain((L, HEAD_DIM)),
        'dil_k_norm': gain((L, HEAD_DIM)),
        'w_br_gla': nrm((L, GLA_V, D_MODEL), GLA_V),
        'w_br_dil': nrm((L, DIL_OUT, D_MODEL), DIL_OUT),
        'w_merge_gate': nrm((L, D_MODEL, 2 * D_MODEL), D_MODEL),
        'b_merge_gate': bias((L, 2 * D_MODEL)),
        'w_mix_out': nrm((L, D_MODEL, D_MODEL), D_MODEL),
        'norm_x': gain((L, D_MODEL)),
        'norm_mem': gain((L, D_MODEL)),
        'w_xq': nrm((L, D_MODEL, D_MODEL), D_MODEL),
        'w_xkv': nrm((L, D_MODEL, 2 * D_MODEL), D_MODEL),
        'x_q_norm': gain((L, X_HEAD_DIM)),
        'x_k_norm': gain((L, X_HEAD_DIM)),
        'w_xo': nrm((L, D_MODEL, D_MODEL), D_MODEL),
        'norm_ffn': gain((L, D_MODEL)),
        'w_ffn_up': nrm((L, D_MODEL, 2 * D_FF), D_MODEL),
        'w_ffn_conv': nrm((L, CONV_W, D_FF), CONV_W),
        'b_ffn_conv': bias((L, D_FF)),
        'w_ffn_down': nrm((L, D_FF, D_MODEL), D_FF),
    }


def reference(x, mem, positions, norm_mix, w_in, w_gla_gate, b_gla_gate, gla_out_norm, dil_q_norm, dil_k_norm,
              w_br_gla, w_br_dil, w_merge_gate, b_merge_gate, w_mix_out, norm_x, norm_mem, w_xq, w_xkv,
              x_q_norm, x_k_norm, w_xo, norm_ffn, w_ffn_up, w_ffn_conv, b_ffn_conv, w_ffn_down):
    inv_freq = ROPE_THETA ** (-jnp.arange(0, ROT_DIM, 2, dtype=jnp.float32) / ROT_DIM)
    ang = positions.astype(jnp.float32)[..., None] * inv_freq
    cos, sin = jnp.cos(ang), jnp.sin(ang)
    h = x
    for l in range(DEPTH):
        h = mixer_sublayer(h, cos, sin, norm_mix[l], w_in[l], w_gla_gate[l], b_gla_gate[l], gla_out_norm[l],
                           dil_q_norm[l], dil_k_norm[l], w_br_gla[l], w_br_dil[l], w_merge_gate[l],
                           b_merge_gate[l], w_mix_out[l])
        h = memory_cross_attention(h, mem, norm_x[l], norm_mem[l], w_xq[l], w_xkv[l], x_q_norm[l],
                                   x_k_norm[l], w_xo[l])
        h = conv_glu_ffn(h, norm_ffn[l], w_ffn_up[l], w_ffn_conv[l], b_ffn_conv[l], w_ffn_down[l])
    return h
```

```python
import functools

import jax
import jax.numpy as jnp
from jax import lax
from jax.experimental import pallas as pl
from jax.experimental.pallas import tpu as pltpu

F32 = jnp.float32
BF16 = jnp.bfloat16

D_MODEL = 1024
EPS = 1e-6
GLA_HEADS = 4
GLA_DK = 128
GLA_DV = 256
GLA_QK = GLA_HEADS * GLA_DK
GLA_V = GLA_HEADS * GLA_DV
GLA_GATE_RANK = 16
GLA_TAU = 16.0
GLA_CHUNK = 64
DIL_GROUPS = ((128, 1), (512, 4), (2048, 16))
DIL_HEADS = 8
HEAD_DIM = 64
DIL_GW = DIL_HEADS * HEAD_DIM
DIL_W = len(DIL_GROUPS) * DIL_GW
DIL_BLOCK = 128
ROT_DIM = HEAD_DIM // 4
ROPE_THETA = 500000.0
X_HEADS = 4
X_HEAD_DIM = D_MODEL // X_HEADS
D_FF = 2816
CONV_W = 3
IN_SIZES = (GLA_QK, GLA_QK, GLA_V, GLA_V, GLA_GATE_RANK, DIL_W, DIL_W, DIL_W)

LANES = 128
VMEM_LIMIT = 56 * 1024 * 1024

TM_PROJ = 256
TS_GLA = 512
TM_MERGE = 512
TM_XATTN = 512
TM_FFN = 512
FF_CHUNK = 256


def _params(sem):
    return pltpu.CompilerParams(dimension_semantics=sem, vmem_limit_bytes=VMEM_LIMIT)


def _resident(shape):
    nd = len(shape)
    return pl.BlockSpec(shape, lambda *_: (0,) * nd, pipeline_mode=pl.Buffered(1))


def _rms(x, gain):
    return x * lax.rsqrt(jnp.mean(x * x, axis=-1, keepdims=True) + EPS) * gain


def _rope_kernel(pos_ref, invf_ref, c_ref, sa_ref, sb_ref):
    ang = pos_ref[...].astype(F32) * invf_ref[...]
    lane = lax.broadcasted_iota(jnp.int32, ang.shape, 1) & (HEAD_DIM - 1)
    cos, sin = jnp.cos(ang), jnp.sin(ang)
    half = ROT_DIM // 2
    c_ref[...] = jnp.where(lane < ROT_DIM, cos, 1.0)
    sa_ref[...] = jnp.where(lane < half, -sin, 0.0)
    sb_ref[...] = jnp.where((lane >= half) & (lane < ROT_DIM), sin, 0.0)


def _rope_tables(positions):
    t = positions.size
    tm = 512
    inv_freq = ROPE_THETA ** (-jnp.arange(0, ROT_DIM, 2, dtype=F32) / ROT_DIM)
    lane = jnp.arange(LANES) % HEAD_DIM
    invf = jnp.where(lane < ROT_DIM, inv_freq[lane % (ROT_DIM // 2)], 0.0).reshape(1, LANES)
    out = jax.ShapeDtypeStruct((t, LANES), F32)
    row = pl.BlockSpec((tm, LANES), lambda i: (i, 0))
    return pl.pallas_call(
        _rope_kernel, grid=(t // tm,),
        in_specs=[pl.BlockSpec((tm, 1), lambda i: (i, 0)), pl.BlockSpec((1, LANES), lambda i: (0, 0))],
        out_specs=[row, row, row], out_shape=[out, out, out],
        compiler_params=_params(("parallel",)), name="rope_tables",
    )(positions.reshape(t, 1), invf)


def _inproj_kernel(x_ref, c_ref, sa_ref, sb_ref, nmix_ref, wqa_ref, wka_ref, wva_ref, wra_ref, wza_ref,
                   wqb_ref, wkb_ref, wvb_ref, wgg_ref, bgg_ref, qn_ref, kn_ref, bd_ref,
                   qa_ref, ka_ref, va_ref, ra_ref, la_ref, qb_ref, kb_ref, vb_ref):
    xn = _rms(x_ref[...], nmix_ref[...]).astype(BF16)

    def proj(w_ref):
        return jnp.dot(xn, w_ref[...], preferred_element_type=F32)

    qa_ref[...] = proj(wqa_ref).astype(BF16)
    ka_ref[...] = proj(wka_ref).astype(BF16)
    va_ref[...] = proj(wva_ref).astype(BF16)
    ra_ref[...] = jax.nn.silu(proj(wra_ref)).astype(BF16)
    za = proj(wza_ref).astype(BF16)
    gate = jnp.dot(za, wgg_ref[...], preferred_element_type=F32) + bgg_ref[...]
    la_ref[...] = jax.nn.log_sigmoid(gate) * (1.0 / GLA_TAU)
    vb_ref[...] = proj(wvb_ref).astype(BF16)

    cos, sa, sb = c_ref[...], sa_ref[...], sb_ref[...]
    bd = bd_ref[...]

    def norm_rope(w_ref, gain_ref, o_ref):
        y = proj(w_ref)
        gain = gain_ref[...]
        for j in range(DIL_W // LANES):
            yc = y[:, j * LANES:(j + 1) * LANES]
            ss = jnp.dot((yc * yc).astype(BF16), bd, preferred_element_type=F32)
            yn = yc * lax.rsqrt(ss * (1.0 / HEAD_DIM) + EPS) * gain
            rot = yn * cos + pltpu.roll(yn, LANES - ROT_DIM // 2, axis=1) * sa + pltpu.roll(yn, ROT_DIM // 2, axis=1) * sb
            o_ref[:, j * LANES:(j + 1) * LANES] = rot.astype(BF16)

    norm_rope(wqb_ref, qn_ref, qb_ref)
    norm_rope(wkb_ref, kn_ref, kb_ref)


def _in_projection(x2, rope, norm_mix, w_in, w_gla_gate, b_gla_gate, dil_q_norm, dil_k_norm):
    t = x2.shape[0]
    tm = TM_PROJ
    offs = [0]
    for s in IN_SIZES:
        offs.append(offs[-1] + s)
    wqa, wka, wva, wra, wza, wqb, wkb, wvb = (w_in[:, offs[i]:offs[i + 1]].astype(BF16) for i in range(8))
    wza = jnp.pad(wza, ((0, 0), (0, LANES - GLA_GATE_RANK)))
    wgg = jnp.pad(w_gla_gate.astype(BF16), ((0, LANES - GLA_GATE_RANK), (0, 0)))
    lane = jnp.arange(LANES)
    bd = (lane[:, None] // HEAD_DIM == lane[None, :] // HEAD_DIM).astype(BF16)
    qn = jnp.tile(dil_q_norm, LANES // HEAD_DIM).reshape(1, LANES)
    kn = jnp.tile(dil_k_norm, LANES // HEAD_DIM).reshape(1, LANES)
    weights = [norm_mix.reshape(1, D_MODEL), wqa, wka, wva, wra, wza, wqb, wkb, wvb, wgg,
               b_gla_gate.reshape(1, GLA_QK), qn, kn, bd]

    def row(w):
        return pl.BlockSpec((tm, w), lambda i: (i, 0))

    out_w = [(GLA_QK, BF16), (GLA_QK, BF16), (GLA_V, BF16), (GLA_V, BF16), (GLA_QK, F32),
             (DIL_W, BF16), (DIL_W, BF16), (DIL_W, BF16)]
    return pl.pallas_call(
        _inproj_kernel, grid=(t // tm,),
        in_specs=[row(D_MODEL), row(LANES), row(LANES), row(LANES)] + [_resident(w.shape) for w in weights],
        out_specs=[row(w) for w, _ in out_w],
        out_shape=[jax.ShapeDtypeStruct((t, w), dt) for w, dt in out_w],
        compiler_params=_params(("parallel",)), name="in_projection",
    )(x2, *rope, *weights)


def _gla_level_refs(g):
    c = GLA_CHUNK
    refs = []
    for h in (32, 16, 8):
        parts = [jnp.broadcast_to(g[m:m + 1, :], (2 * h, g.shape[1])) for m in range(h, c, 2 * h)]
        refs.append(parts[0] if len(parts) == 1 else jnp.concatenate(parts, axis=0))
    g3 = g.reshape(c // 8, 8, g.shape[1])
    sub = lax.broadcasted_iota(jnp.int32, g3.shape, 1)

    def pick(s):
        return jnp.broadcast_to(g3[:, s:s + 1, :], g3.shape)

    refs.append(pick(4).reshape(g.shape))
    refs.append(jnp.where(sub < 4, pick(2), pick(6)).reshape(g.shape))
    r1 = jnp.where(sub < 2, pick(1), jnp.where(sub < 4, pick(3), jnp.where(sub < 6, pick(5), pick(7))))
    refs.append(r1.reshape(g.shape))
    return refs


def _gla_kernel(qa_ref, ka_ref, va_ref, ra_ref, la_ref, gn_ref, o_ref, state_ref):
    c = GLA_CHUNK

    @pl.when(pl.program_id(1) == 0)
    def _():
        state_ref[...] = jnp.zeros_like(state_ref)

    ri = lax.broadcasted_iota(jnp.int32, (c, c), 0)
    ci = lax.broadcasted_iota(jnp.int32, (c, c), 1)
    tril = (ri >= ci).astype(F32)
    row = lax.broadcasted_iota(jnp.int32, (c, GLA_DK), 0)
    bits = (5, 4, 3, 2, 1, 0)
    upper = [((row >> b) & 1) == 1 for b in bits]
    pair_masks = [((ri >> (b + 1)) == (ci >> (b + 1))) & (((ri >> b) & 1) == 1) & (((ci >> b) & 1) == 0)
                  for b in bits]
    pair_masks.append(ri == ci)
    gn = gn_ref[...]

    def chunk(ic, carry):
        rows = pl.ds(pl.multiple_of(ic * c, c), c)
        la = la_ref[rows, :]
        g_all = jnp.dot(tril, la, precision=lax.Precision.HIGHEST, preferred_element_type=F32)
        for h in range(GLA_HEADS):
            ks = slice(h * GLA_DK, (h + 1) * GLA_DK)
            vs = slice(h * GLA_DV, (h + 1) * GLA_DV)
            q = qa_ref[rows, ks].astype(F32) * (GLA_DK ** -0.5)
            k = ka_ref[rows, ks].astype(F32)
            v = va_ref[rows, vs]
            g = g_all[:, ks]
            st = state_ref[h]
            o_inter = lax.dot_general((q * jnp.exp(g)).astype(BF16), st.astype(BF16),
                                      (((1,), (1,)), ((), ())), preferred_element_type=F32)
            qs, kss = [], []
            for up, gm in zip(upper, _gla_level_refs(g)):
                e = jnp.exp(jnp.where(up, g - gm, gm - g))
                qs.append(jnp.where(up, q * e, 0.0).astype(BF16))
                kss.append(jnp.where(up, 0.0, k * e).astype(BF16))
            qs.append(q.astype(BF16))
            kss.append(k.astype(BF16))
            p = jnp.einsum('lik,ljk->lij', jnp.stack(qs), jnp.stack(kss), preferred_element_type=F32)
            attn = jnp.zeros((c, c), F32)
            for l, m in enumerate(pair_masks):
                attn = attn + jnp.where(m, p[l], 0.0)
            o = o_inter + jnp.dot(attn.astype(BF16), v, preferred_element_type=F32)
            g_last = g[c - 1:c, :]
            kd = (k * jnp.exp(g_last - g)).astype(BF16)
            state_ref[h] = st * jnp.exp(g_last) + lax.dot_general(
                v, kd, (((0,), (0,)), ((), ())), preferred_element_type=F32)
            o = _rms(o, gn)
            o_ref[rows, vs] = (o.astype(F32) * ra_ref[rows, vs].astype(F32)).astype(BF16)
        return carry

    lax.fori_loop(0, qa_ref.shape[0] // c, chunk, 0)


def _gla(qa, ka, va, ra, la, gla_out_norm, batch, seq):
    ts = TS_GLA
    ns = seq // ts

    def row(w):
        return pl.BlockSpec((ts, w), lambda b, s: (b * ns + s, 0))

    return pl.pallas_call(
        _gla_kernel, grid=(batch, ns),
        in_specs=[row(GLA_QK), row(GLA_QK), row(GLA_V), row(GLA_V), row(GLA_QK),
                  pl.BlockSpec((1, GLA_DV), lambda b, s: (0, 0))],
        out_specs=row(GLA_V),
        out_shape=jax.ShapeDtypeStruct((batch * seq, GLA_V), BF16),
        scratch_shapes=[pltpu.VMEM((GLA_HEADS, GLA_DV, GLA_DK), F32)],
        compiler_params=_params(("parallel", "arbitrary")), name="gla",
    )(qa, ka, va, ra, la, gla_out_norm.reshape(1, GLA_DV))


def _dil_kernel(q_ref, kp_ref, kc_ref, vp_ref, vc_ref, o_ref, lse_ref):
    blk = DIL_BLOCK
    has_prev = pl.program_id(2) > 0
    qi = lax.broadcasted_iota(jnp.int32, (blk, 2 * blk), 0)
    kj = lax.broadcasted_iota(jnp.int32, (blk, 2 * blk), 1)
    dist = qi + blk - kj
    valid = (dist >= 0) & (dist <= blk) & ((kj >= blk) | has_prev)
    lane = lax.broadcasted_iota(jnp.int32, (blk, LANES), 1)
    for p in range(DIL_GW // LANES):
        cs = slice(p * LANES, (p + 1) * LANES)
        q2 = q_ref[:, cs]
        kk = jnp.concatenate([kp_ref[:, cs], kc_ref[:, cs]], axis=0)
        vv = jnp.concatenate([vp_ref[:, cs], vc_ref[:, cs]], axis=0)
        outs, lses = [], []
        for hh in range(LANES // HEAD_DIM):
            mine = (lane < HEAD_DIM) if hh == 0 else (lane >= HEAD_DIM)
            qm = jnp.where(mine, q2, jnp.zeros_like(q2))
            s = lax.dot_general(qm, kk, (((1,), (1,)), ((), ())), preferred_element_type=F32) * (HEAD_DIM ** -0.5)
            s = jnp.where(valid, s, -jnp.inf)
            m = jnp.max(s, axis=-1, keepdims=True)
            e = jnp.exp(s - m)
            l = jnp.sum(e, axis=-1, keepdims=True)
            outs.append(jnp.dot(e.astype(BF16), vv, preferred_element_type=F32) / l)
            lses.append(m + jnp.log(l))
        o_ref[:, cs] = jnp.where(lane < HEAD_DIM, outs[0], outs[1]).astype(BF16)
        lse_ref[:, cs] = jnp.where(lane < HEAD_DIM, lses[0], lses[1])


def _dilated_group(q, k, v, batch, seq, dil):
    length = seq // dil
    nb = length // DIL_BLOCK
    view = (batch, length, dil * DIL_GW)

    def spec(idx):
        return pl.BlockSpec((None, DIL_BLOCK, DIL_GW), idx)

    cur = spec(lambda b, r, n: (b, n, r))
    prev = spec(lambda b, r, n: (b, jnp.maximum(n - 1, 0), r))
    qv, kv, vv = (a.reshape(view) for a in (q, k, v))
    o, lse = pl.pallas_call(
        _dil_kernel, grid=(batch, dil, nb),
        in_specs=[cur, prev, cur, prev, cur], out_specs=[cur, cur],
        out_shape=[jax.ShapeDtypeStruct(view, BF16), jax.ShapeDtypeStruct(view, F32)],
        compiler_params=_params(("parallel", "parallel", "arbitrary")), name=f"dilated_attn_d{dil}",
    )(qv, kv, kv, vv, vv)
    return o.reshape(batch * seq, DIL_GW), lse.reshape(batch * seq, DIL_GW)


def _merge_kernel(x_ref, oa_ref, o0_ref, o1_ref, o2_ref, l0_ref, l1_ref, l2_ref, nmix_ref, wmg_ref, bmg_ref,
                  wbg_ref, wbd_ref, wmo_ref, h_ref):
    x = x_ref[...]
    xn = _rms(x, nmix_ref[...]).astype(BF16)
    gates = jax.nn.sigmoid(jnp.dot(xn, wmg_ref[...], preferred_element_type=F32) + bmg_ref[...])
    l0, l1, l2 = l0_ref[...], l1_ref[...], l2_ref[...]
    m = jnp.maximum(jnp.maximum(l0, l1), l2)
    e0, e1, e2 = jnp.exp(l0 - m), jnp.exp(l1 - m), jnp.exp(l2 - m)
    ob = (e0 * o0_ref[...].astype(F32) + e1 * o1_ref[...].astype(F32) + e2 * o2_ref[...].astype(F32)) / (e0 + e1 + e2)
    br_a = jnp.dot(oa_ref[...], wbg_ref[...], preferred_element_type=F32)
    br_b = jnp.dot(ob.astype(BF16), wbd_ref[...], preferred_element_type=F32)
    merged = gates[:, :D_MODEL] * br_a + gates[:, D_MODEL:] * br_b
    h_ref[...] = x + jnp.dot(merged.astype(BF16), wmo_ref[...], preferred_element_type=F32)


def _merge(x2, oa, outs, lses, norm_mix, w_merge_gate, b_merge_gate, w_br_gla, w_br_dil, w_mix_out):
    t = x2.shape[0]
    tm = TM_MERGE
    weights = [norm_mix.reshape(1, D_MODEL), w_merge_gate.astype(BF16), b_merge_gate.reshape(1, 2 * D_MODEL),
               w_br_gla.astype(BF16), w_br_dil.astype(BF16), w_mix_out.astype(BF16)]

    def row(w):
        return pl.BlockSpec((tm, w), lambda i: (i, 0))

    return pl.pallas_call(
        _merge_kernel, grid=(t // tm,),
        in_specs=[row(D_MODEL), row(GLA_V)] + [row(DIL_GW)] * 6 + [_resident(w.shape) for w in weights],
        out_specs=row(D_MODEL), out_shape=jax.ShapeDtypeStruct((t, D_MODEL), F32),
        compiler_params=_params(("parallel",)), name="merge",
    )(x2, oa, *outs, *lses, *weights)


def _memkv_kernel(mem_ref, nmem_ref, wkv_ref, kn_ref, k_ref, v_ref):
    mn = _rms(mem_ref[...], nmem_ref[...]).astype(BF16)
    kv = jnp.dot(mn, wkv_ref[...], preferred_element_type=F32)
    kn = kn_ref[...]
    for h in range(X_HEADS):
        cs = slice(h * X_HEAD_DIM, (h + 1) * X_HEAD_DIM)
        k_ref[:, cs] = _rms(kv[:, cs], kn).astype(BF16)
    v_ref[...] = kv[:, D_MODEL:].astype(BF16)


def _mem_kv(mem, norm_mem, w_xkv, x_k_norm):
    batch, n_mem, _ = mem.shape
    blk = pl.BlockSpec((None, n_mem, D_MODEL), lambda b: (b, 0, 0))
    weights = [norm_mem.reshape(1, D_MODEL), w_xkv.astype(BF16), x_k_norm.reshape(1, X_HEAD_DIM)]
    out = jax.ShapeDtypeStruct((batch, n_mem, D_MODEL), BF16)
    return pl.pallas_call(
        _memkv_kernel, grid=(batch,),
        in_specs=[blk] + [_resident(w.shape) for w in weights],
        out_specs=[blk, blk], out_shape=[out, out],
        compiler_params=_params(("parallel",)), name="mem_kv",
    )(mem, *weights)


def _xattn_kernel(h_ref, k_ref, v_ref, nx_ref, wq_ref, qn_ref, wo_ref, o_ref):
    h = h_ref[...]
    xn = _rms(h, nx_ref[...]).astype(BF16)
    q = jnp.dot(xn, wq_ref[...], preferred_element_type=F32)
    qn = qn_ref[...]
    outs = []
    for hd in range(X_HEADS):
        cs = slice(hd * X_HEAD_DIM, (hd + 1) * X_HEAD_DIM)
        qh = _rms(q[:, cs], qn).astype(BF16)
        s = lax.dot_general(qh, k_ref[:, cs], (((1,), (1,)), ((), ())), preferred_element_type=F32)
        s = s * (X_HEAD_DIM ** -0.5)
        e = jnp.exp(s - jnp.max(s, axis=-1, keepdims=True))
        l = jnp.sum(e, axis=-1, keepdims=True)
        outs.append((jnp.dot(e.astype(BF16), v_ref[:, cs], preferred_element_type=F32) / l).astype(BF16))
    o = jnp.concatenate(outs, axis=-1)
    o_ref[...] = h + jnp.dot(o, wo_ref[...], preferred_element_type=F32)


def _cross_attention(h1, kmem, vmem, norm_x, w_xq, x_q_norm, w_xo, batch, seq):
    tm = TM_XATTN
    ns = seq // tm
    n_mem = kmem.shape[1]
    weights_a = [norm_x.reshape(1, D_MODEL), w_xq.astype(BF16), x_q_norm.reshape(1, X_HEAD_DIM), w_xo.astype(BF16)]
    row = pl.BlockSpec((tm, D_MODEL), lambda i: (i, 0))
    memblk = pl.BlockSpec((None, n_mem, D_MODEL), lambda i: (i // ns, 0, 0))
    return pl.pallas_call(
        _xattn_kernel, grid=(batch * ns,),
        in_specs=[row, memblk, memblk] + [_resident(w.shape) for w in weights_a],
        out_specs=row, out_shape=jax.ShapeDtypeStruct(h1.shape, F32),
        compiler_params=_params(("parallel",)), name="cross_attention",
    )(h1, kmem, vmem, *weights_a)


def _ffn_kernel(h_ref, nf_ref, wa_ref, wu_ref, wc_ref, bc_ref, wd_ref, o_ref, abuf_ref):
    tm = h_ref.shape[0]
    pad = 8

    @pl.when(pl.program_id(1) == 0)
    def _():
        abuf_ref[tm:tm + pad, :] = jnp.zeros((pad, D_FF), F32)

    h = h_ref[...]
    xn = _rms(h, nf_ref[...]).astype(BF16)
    acc = jnp.zeros((tm, D_MODEL), F32)
    for j in range(D_FF // FF_CHUNK):
        cs = slice(j * FF_CHUNK, (j + 1) * FF_CHUNK)
        a = jnp.dot(xn, wa_ref[:, cs], preferred_element_type=F32)
        u = jnp.dot(xn, wu_ref[:, cs], preferred_element_type=F32)
        abuf_ref[0:pad, cs] = abuf_ref[tm:tm + pad, cs]
        abuf_ref[pad:pad + tm, cs] = a
        conv = bc_ref[:, cs] + a * wc_ref[2:3, cs]
        conv = conv + abuf_ref[pad - 1:pad - 1 + tm, cs] * wc_ref[1:2, cs]
        conv = conv + abuf_ref[pad - 2:pad - 2 + tm, cs] * wc_ref[0:1, cs]
        gelu = 0.5 * conv * (1.0 + lax.erf(conv * (2.0 ** -0.5)))
        acc = acc + jnp.dot((gelu * u).astype(BF16), wd_ref[cs, :], preferred_element_type=F32)
    o_ref[...] = h + acc


def _ffn(h2, norm_ffn, w_ffn_up, w_ffn_conv, b_ffn_conv, w_ffn_down, batch, seq):
    tm = TM_FFN
    ns = seq // tm
    weights = [norm_ffn.reshape(1, D_MODEL), w_ffn_up[:, :D_FF].astype(BF16), w_ffn_up[:, D_FF:].astype(BF16),
               w_ffn_conv, b_ffn_conv.reshape(1, D_FF), w_ffn_down.astype(BF16)]
    row = pl.BlockSpec((tm, D_MODEL), lambda b, s: (b * ns + s, 0))
    return pl.pallas_call(
        _ffn_kernel, grid=(batch, ns),
        in_specs=[row] + [_resident(w.shape) for w in weights],
        out_specs=row, out_shape=jax.ShapeDtypeStruct(h2.shape, F32),
        scratch_shapes=[pltpu.VMEM((tm + 8, D_FF), F32)],
        compiler_params=_params(("parallel", "arbitrary")), name="conv_glu_ffn",
    )(h2, *weights)


def _layer(h, rope, mem, batch, seq, norm_mix, w_in, w_gla_gate, b_gla_gate, gla_out_norm, dil_q_norm, dil_k_norm,
           w_br_gla, w_br_dil, w_merge_gate, b_merge_gate, w_mix_out, norm_x, norm_mem, w_xq, w_xkv, x_q_norm,
           x_k_norm, w_xo, norm_ffn, w_ffn_up, w_ffn_conv, b_ffn_conv, w_ffn_down):
    qa, ka, va, ra, la, qb, kb, vb = _in_projection(h, rope, norm_mix, w_in, w_gla_gate, b_gla_gate,
                                                    dil_q_norm, dil_k_norm)
    oa = _gla(qa, ka, va, ra, la, gla_out_norm, batch, seq)
    outs, lses = [], []
    for gi, (_, dil) in enumerate(DIL_GROUPS):
        cs = slice(gi * DIL_GW, (gi + 1) * DIL_GW)
        o, lse = _dilated_group(qb[:, cs], kb[:, cs], vb[:, cs], batch, seq, dil)
        outs.append(o)
        lses.append(lse)
    h1 = _merge(h, oa, outs, lses, norm_mix, w_merge_gate, b_merge_gate, w_br_gla, w_br_dil, w_mix_out)
    kmem, vmem = _mem_kv(mem, norm_mem, w_xkv, x_k_norm)
    h2 = _cross_attention(h1, kmem, vmem, norm_x, w_xq, x_q_norm, w_xo, batch, seq)
    return _ffn(h2, norm_ffn, w_ffn_up, w_ffn_conv, b_ffn_conv, w_ffn_down, batch, seq)


def kernel(x, mem, positions, norm_mix, w_in, w_gla_gate, b_gla_gate, gla_out_norm, dil_q_norm, dil_k_norm, w_br_gla, w_br_dil, w_merge_gate, b_merge_gate, w_mix_out, norm_x, norm_mem, w_xq, w_xkv, x_q_norm, x_k_norm, w_xo, norm_ffn, w_ffn_up, w_ffn_conv, b_ffn_conv, w_ffn_down):
    batch, seq, _ = x.shape
    rope = _rope_tables(positions)
    h = x.reshape(batch * seq, D_MODEL)
    stacked = (norm_mix, w_in, w_gla_gate, b_gla_gate, gla_out_norm, dil_q_norm, dil_k_norm, w_br_gla, w_br_dil,
               w_merge_gate, b_merge_gate, w_mix_out, norm_x, norm_mem, w_xq, w_xkv, x_q_norm, x_k_norm, w_xo,
               norm_ffn, w_ffn_up, w_ffn_conv, b_ffn_conv, w_ffn_down)
    for l in range(norm_mix.shape[0]):
        h = _layer(h, rope, mem, batch, seq, *(p[l] for p in stacked))
    return h.reshape(batch, seq, D_MODEL)
```

```python
import functools

import jax
import jax.numpy as jnp
from jax import lax
from jax.experimental import pallas as pl
from jax.experimental.pallas import tpu as pltpu

F32 = jnp.float32
BF16 = jnp.bfloat16

D_MODEL = 1024
EPS = 1e-6
GLA_HEADS = 4
GLA_DK = 128
GLA_DV = 256
GLA_QK = GLA_HEADS * GLA_DK
GLA_V = GLA_HEADS * GLA_DV
GLA_GATE_RANK = 16
GLA_TAU = 16.0
GLA_CHUNK = 64
DIL_GROUPS = ((128, 1), (512, 4), (2048, 16))
DIL_HEADS = 8
HEAD_DIM = 64
DIL_GW = DIL_HEADS * HEAD_DIM
DIL_W = len(DIL_GROUPS) * DIL_GW
DIL_BLOCK = 128
ROT_DIM = HEAD_DIM // 4
ROPE_THETA = 500000.0
X_HEADS = 4
X_HEAD_DIM = D_MODEL // X_HEADS
D_FF = 2816
CONV_W = 3
IN_SIZES = (GLA_QK, GLA_QK, GLA_V, GLA_V, GLA_GATE_RANK, DIL_W, DIL_W, DIL_W)

LANES = 128
VMEM_LIMIT = 56 * 1024 * 1024

TM_PROJ = 256
TS_GLA = 512
TM_MERGE = 512
TM_XATTN = 512
TM_FFN = 512
FF_CHUNK = 256


def _params(sem):
    return pltpu.CompilerParams(dimension_semantics=sem, vmem_limit_bytes=VMEM_LIMIT)


def _resident(shape):
    nd = len(shape)
    return pl.BlockSpec(shape, lambda *_: (0,) * nd, pipeline_mode=pl.Buffered(1))


def _rms(x, gain):
    return x * lax.rsqrt(jnp.mean(x * x, axis=-1, keepdims=True) + EPS) * gain


def _rope_kernel(pos_ref, invf_ref, c_ref, sa_ref, sb_ref):
    ang = pos_ref[...].astype(F32) * invf_ref[...]
    lane = lax.broadcasted_iota(jnp.int32, ang.shape, 1) & (HEAD_DIM - 1)
    cos, sin = jnp.cos(ang), jnp.sin(ang)
    half = ROT_DIM // 2
    c_ref[...] = jnp.where(lane < ROT_DIM, cos, 1.0)
    sa_ref[...] = jnp.where(lane < half, -sin, 0.0)
    sb_ref[...] = jnp.where((lane >= half) & (lane < ROT_DIM), sin, 0.0)


def _rope_tables(positions):
    t = positions.size
    tm = 512
    inv_freq = ROPE_THETA ** (-jnp.arange(0, ROT_DIM, 2, dtype=F32) / ROT_DIM)
    lane = jnp.arange(LANES) % HEAD_DIM
    invf = jnp.where(lane < ROT_DIM, inv_freq[lane % (ROT_DIM // 2)], 0.0).reshape(1, LANES)
    out = jax.ShapeDtypeStruct((t, LANES), F32)
    row = pl.BlockSpec((tm, LANES), lambda i: (i, 0))
    return pl.pallas_call(
        _rope_kernel, grid=(t // tm,),
        in_specs=[pl.BlockSpec((tm, 1), lambda i: (i, 0)), pl.BlockSpec((1, LANES), lambda i: (0, 0))],
        out_specs=[row, row, row], out_shape=[out, out, out],
        compiler_params=_params(("parallel",)), name="rope_tables",
    )(positions.reshape(t, 1), invf)


def _store_residue_major(o_ref, col, val, scr_ref, dil):
    cs = slice(col, col + LANES)
    if dil == 1:
        o_ref[0, :, cs] = val.astype(BF16)
        return
    n = val.shape[0] // dil
    scr_ref[...] = val
    for r in range(dil):
        o_ref[r, :, cs] = scr_ref[pl.ds(r, n, stride=dil), :].astype(BF16)


def _inproj_kernel(x_ref, c_ref, sa_ref, sb_ref, nmix_ref, wqa_ref, wka_ref, wva_ref, wra_ref, wza_ref,
                   wqb_ref, wkb_ref, wvb_ref, wgg_ref, bgg_ref, qn_ref, kn_ref, bd_ref,
                   qa_ref, ka_ref, va_ref, ra_ref, la_ref, *rest):
    n_groups = len(DIL_GROUPS)
    q_refs, k_refs, v_refs = rest[:n_groups], rest[n_groups:2 * n_groups], rest[2 * n_groups:3 * n_groups]
    scratch = rest[3 * n_groups:]
    xn = _rms(x_ref[...], nmix_ref[...]).astype(BF16)

    def proj(w_ref):
        return jnp.dot(xn, w_ref[...], preferred_element_type=F32)

    qa_ref[...] = proj(wqa_ref).astype(BF16)
    ka_ref[...] = proj(wka_ref).astype(BF16)
    va_ref[...] = proj(wva_ref).astype(BF16)
    ra_ref[...] = jax.nn.silu(proj(wra_ref)).astype(BF16)
    za = proj(wza_ref).astype(BF16)
    gate = jnp.dot(za, wgg_ref[...], preferred_element_type=F32) + bgg_ref[...]
    la_ref[...] = jax.nn.log_sigmoid(gate) * (1.0 / GLA_TAU)

    cos, sa, sb = c_ref[...], sa_ref[...], sb_ref[...]
    bd = bd_ref[...]
    chunks_per_group = DIL_GW // LANES
    uses = [0]

    def emit(o_refs, j, val):
        gi = j // chunks_per_group
        scr = scratch[uses[0] % len(scratch)]
        uses[0] += 1
        _store_residue_major(o_refs[gi], (j % chunks_per_group) * LANES, val, scr, DIL_GROUPS[gi][1])

    def norm_rope(w_ref, gain_ref, o_refs):
        y = proj(w_ref)
        gain = gain_ref[...]
        for j in range(DIL_W // LANES):
            yc = y[:, j * LANES:(j + 1) * LANES]
            ss = jnp.dot((yc * yc).astype(BF16), bd, preferred_element_type=F32)
            yn = yc * lax.rsqrt(ss * (1.0 / HEAD_DIM) + EPS) * gain
            rot = yn * cos + pltpu.roll(yn, LANES - ROT_DIM // 2, axis=1) * sa + pltpu.roll(yn, ROT_DIM // 2, axis=1) * sb
            emit(o_refs, j, rot)

    norm_rope(wqb_ref, qn_ref, q_refs)
    norm_rope(wkb_ref, kn_ref, k_refs)
    yv = proj(wvb_ref)
    for j in range(DIL_W // LANES):
        emit(v_refs, j, yv[:, j * LANES:(j + 1) * LANES])


def _in_projection(x2, rope, batch, seq, norm_mix, w_in, w_gla_gate, b_gla_gate, dil_q_norm, dil_k_norm):
    tm = TM_PROJ
    ns = seq // tm
    offs = [0]
    for s in IN_SIZES:
        offs.append(offs[-1] + s)
    wqa, wka, wva, wra, wza, wqb, wkb, wvb = (w_in[:, offs[i]:offs[i + 1]].astype(BF16) for i in range(8))
    wza = jnp.pad(wza, ((0, 0), (0, LANES - GLA_GATE_RANK)))
    wgg = jnp.pad(w_gla_gate.astype(BF16), ((0, LANES - GLA_GATE_RANK), (0, 0)))
    lane = jnp.arange(LANES)
    bd = (lane[:, None] // HEAD_DIM == lane[None, :] // HEAD_DIM).astype(BF16)
    qn = jnp.tile(dil_q_norm, LANES // HEAD_DIM).reshape(1, LANES)
    kn = jnp.tile(dil_k_norm, LANES // HEAD_DIM).reshape(1, LANES)
    weights = [norm_mix.reshape(1, D_MODEL), wqa, wka, wva, wra, wza, wqb, wkb, wvb, wgg,
               b_gla_gate.reshape(1, GLA_QK), qn, kn, bd]

    def row(w):
        return pl.BlockSpec((tm, w), lambda b, s: (b * ns + s, 0))

    t = batch * seq
    flat_w = [(GLA_QK, BF16), (GLA_QK, BF16), (GLA_V, BF16), (GLA_V, BF16), (GLA_QK, F32)]
    out_specs = [row(w) for w, _ in flat_w]
    out_shape = [jax.ShapeDtypeStruct((t, w), dt) for w, dt in flat_w]
    for _ in range(3):
        for _, dil in DIL_GROUPS:
            out_specs.append(pl.BlockSpec((None, dil, tm // dil, DIL_GW), lambda b, s: (b, 0, s, 0)))
            out_shape.append(jax.ShapeDtypeStruct((batch, dil, seq // dil, DIL_GW), BF16))
    outs = pl.pallas_call(
        _inproj_kernel, grid=(batch, ns),
        in_specs=[row(D_MODEL), row(LANES), row(LANES), row(LANES)] + [_resident(w.shape) for w in weights],
        out_specs=out_specs, out_shape=out_shape,
        scratch_shapes=[pltpu.VMEM((tm, LANES), F32)] * 4,
        compiler_params=_params(("parallel", "parallel")), name="in_projection",
    )(x2, *rope, *weights)
    n = len(DIL_GROUPS)
    return outs[:5], outs[5:5 + n], outs[5 + n:5 + 2 * n], outs[5 + 2 * n:]


def _gla_level_refs(g):
    c = GLA_CHUNK
    refs = []
    for h in (32, 16, 8):
        parts = [jnp.broadcast_to(g[m:m + 1, :], (2 * h, g.shape[1])) for m in range(h, c, 2 * h)]
        refs.append(parts[0] if len(parts) == 1 else jnp.concatenate(parts, axis=0))
    g3 = g.reshape(c // 8, 8, g.shape[1])
    sub = lax.broadcasted_iota(jnp.int32, g3.shape, 1)

    def pick(s):
        return jnp.broadcast_to(g3[:, s:s + 1, :], g3.shape)

    refs.append(pick(4).reshape(g.shape))
    refs.append(jnp.where(sub < 4, pick(2), pick(6)).reshape(g.shape))
    r1 = jnp.where(sub < 2, pick(1), jnp.where(sub < 4, pick(3), jnp.where(sub < 6, pick(5), pick(7))))
    refs.append(r1.reshape(g.shape))
    return refs


def _gla_kernel(qa_ref, ka_ref, va_ref, ra_ref, la_ref, gn_ref, o_ref, state_ref):
    c = GLA_CHUNK

    @pl.when(pl.program_id(1) == 0)
    def _():
        state_ref[...] = jnp.zeros_like(state_ref)

    ri = lax.broadcasted_iota(jnp.int32, (c, c), 0)
    ci = lax.broadcasted_iota(jnp.int32, (c, c), 1)
    tril = (ri >= ci).astype(F32)
    row = lax.broadcasted_iota(jnp.int32, (c, GLA_DK), 0)
    bits = (5, 4, 3, 2, 1, 0)
    upper = [((row >> b) & 1) == 1 for b in bits]
    pair_masks = [((ri >> (b + 1)) == (ci >> (b + 1))) & (((ri >> b) & 1) == 1) & (((ci >> b) & 1) == 0)
                  for b in bits]
    pair_masks.append(ri == ci)
    gn = gn_ref[...]

    def chunk(ic, carry):
        rows = pl.ds(pl.multiple_of(ic * c, c), c)
        la = la_ref[rows, :]
        g_all = jnp.dot(tril, la, precision=lax.Precision.HIGHEST, preferred_element_type=F32)
        for h in range(GLA_HEADS):
            ks = slice(h * GLA_DK, (h + 1) * GLA_DK)
            vs = slice(h * GLA_DV, (h + 1) * GLA_DV)
            q = qa_ref[rows, ks].astype(F32) * (GLA_DK ** -0.5)
            k = ka_ref[rows, ks].astype(F32)
            v = va_ref[rows, vs]
            g = g_all[:, ks]
            st = state_ref[h]
            o_inter = lax.dot_general((q * jnp.exp(g)).astype(BF16), st.astype(BF16),
                                      (((1,), (1,)), ((), ())), preferred_element_type=F32)
            qs, kss = [], []
            for up, gm in zip(upper, _gla_level_refs(g)):
                e = jnp.exp(jnp.where(up, g - gm, gm - g))
                qs.append(jnp.where(up, q * e, 0.0).astype(BF16))
                kss.append(jnp.where(up, 0.0, k * e).astype(BF16))
            qs.append(q.astype(BF16))
            kss.append(k.astype(BF16))
            p = jnp.einsum('lik,ljk->lij', jnp.stack(qs), jnp.stack(kss), preferred_element_type=F32)
            attn = jnp.zeros((c, c), F32)
            for l, m in enumerate(pair_masks):
                attn = attn + jnp.where(m, p[l], 0.0)
            o = o_inter + jnp.dot(attn.astype(BF16), v, preferred_element_type=F32)
            g_last = g[c - 1:c, :]
            kd = (k * jnp.exp(g_last - g)).astype(BF16)
            state_ref[h] = st * jnp.exp(g_last) + lax.dot_general(
                v, kd, (((0,), (0,)), ((), ())), preferred_element_type=F32)
            o = _rms(o, gn)
            o_ref[rows, vs] = (o.astype(F32) * ra_ref[rows, vs].astype(F32)).astype(BF16)
        return carry

    lax.fori_loop(0, qa_ref.shape[0] // c, chunk, 0)


def _gla(qa, ka, va, ra, la, gla_out_norm, batch, seq):
    ts = TS_GLA
    ns = seq // ts

    def row(w):
        return pl.BlockSpec((ts, w), lambda b, s: (b * ns + s, 0))

    return pl.pallas_call(
        _gla_kernel, grid=(batch, ns),
        in_specs=[row(GLA_QK), row(GLA_QK), row(GLA_V), row(GLA_V), row(GLA_QK),
                  pl.BlockSpec((1, GLA_DV), lambda b, s: (0, 0))],
        out_specs=row(GLA_V),
        out_shape=jax.ShapeDtypeStruct((batch * seq, GLA_V), BF16),
        scratch_shapes=[pltpu.VMEM((GLA_HEADS, GLA_DV, GLA_DK), F32)],
        compiler_params=_params(("parallel", "arbitrary")), name="gla",
    )(qa, ka, va, ra, la, gla_out_norm.reshape(1, GLA_DV))


def _dil_kernel(q_ref, kp_ref, kc_ref, vp_ref, vc_ref, o_ref, lse_ref):
    blk = DIL_BLOCK
    has_prev = pl.program_id(2) > 0
    qi = lax.broadcasted_iota(jnp.int32, (blk, 2 * blk), 0)
    kj = lax.broadcasted_iota(jnp.int32, (blk, 2 * blk), 1)
    dist = qi + blk - kj
    valid = (dist >= 0) & (dist <= blk) & ((kj >= blk) | has_prev)
    lane = lax.broadcasted_iota(jnp.int32, (blk, LANES), 1)
    for p in range(DIL_GW // LANES):
        cs = slice(p * LANES, (p + 1) * LANES)
        q2 = q_ref[:, cs]
        kk = jnp.concatenate([kp_ref[:, cs], kc_ref[:, cs]], axis=0)
        vv = jnp.concatenate([vp_ref[:, cs], vc_ref[:, cs]], axis=0)
        outs, lses = [], []
        for hh in range(LANES // HEAD_DIM):
            mine = (lane < HEAD_DIM) if hh == 0 else (lane >= HEAD_DIM)
            qm = jnp.where(mine, q2, jnp.zeros_like(q2))
            s = lax.dot_general(qm, kk, (((1,), (1,)), ((), ())), preferred_element_type=F32) * (HEAD_DIM ** -0.5)
            s = jnp.where(valid, s, -jnp.inf)
            m = jnp.max(s, axis=-1, keepdims=True)
            e = jnp.exp(s - m)
            l = jnp.sum(e, axis=-1, keepdims=True)
            outs.append(jnp.dot(e.astype(BF16), vv, preferred_element_type=F32) / l)
            lses.append(m + jnp.log(l))
        o_ref[:, cs] = jnp.where(lane < HEAD_DIM, outs[0], outs[1]).astype(BF16)
        lse_ref[:, cs] = jnp.where(lane < HEAD_DIM, lses[0], lses[1])


def _dilated_group(q, k, v):
    batch, dil, length, _ = q.shape
    nb = length // DIL_BLOCK

    def spec(idx):
        return pl.BlockSpec((None, None, DIL_BLOCK, DIL_GW), idx)

    cur = spec(lambda b, r, n: (b, r, n, 0))
    prev = spec(lambda b, r, n: (b, r, jnp.maximum(n - 1, 0), 0))
    return pl.pallas_call(
        _dil_kernel, grid=(batch, dil, nb),
        in_specs=[cur, prev, cur, prev, cur], out_specs=[cur, cur],
        out_shape=[jax.ShapeDtypeStruct(q.shape, BF16), jax.ShapeDtypeStruct(q.shape, F32)],
        compiler_params=_params(("parallel", "parallel", "arbitrary")), name=f"dilated_attn_d{dil}",
    )(q, k, k, v, v)


def _load_token_major(src_ref, col, scr_ref, dil):
    cs = slice(col, col + LANES)
    if dil == 1:
        return src_ref[0, :, cs].astype(F32)
    n = src_ref.shape[1]
    for r in range(dil):
        scr_ref[pl.ds(r, n, stride=dil), :] = src_ref[r, :, cs].astype(F32)
    return scr_ref[...]


def _merge_kernel(x_ref, oa_ref, o0_ref, o1_ref, o2_ref, l0_ref, l1_ref, l2_ref, nmix_ref, wmg_ref, bmg_ref,
                  wbg_ref, wbd_ref, wmo_ref, h_ref, *scratch):
    x = x_ref[...]
    xn = _rms(x, nmix_ref[...]).astype(BF16)
    gates = jax.nn.sigmoid(jnp.dot(xn, wmg_ref[...], preferred_element_type=F32) + bmg_ref[...])
    br_a = jnp.dot(oa_ref[...], wbg_ref[...], preferred_element_type=F32)
    o_refs, l_refs = (o0_ref, o1_ref, o2_ref), (l0_ref, l1_ref, l2_ref)
    br_b = jnp.zeros_like(br_a)
    uses = 0
    for j in range(DIL_GW // LANES):
        ls, os_ = [], []
        for gi, (_, dil) in enumerate(DIL_GROUPS):
            ls.append(_load_token_major(l_refs[gi], j * LANES, scratch[uses % len(scratch)], dil))
            os_.append(_load_token_major(o_refs[gi], j * LANES, scratch[(uses + 1) % len(scratch)], dil))
            uses += 2
        m = jnp.maximum(jnp.maximum(ls[0], ls[1]), ls[2])
        es = [jnp.exp(l - m) for l in ls]
        ob = (es[0] * os_[0] + es[1] * os_[1] + es[2] * os_[2]) / (es[0] + es[1] + es[2])
        br_b = br_b + jnp.dot(ob.astype(BF16), wbd_ref[j * LANES:(j + 1) * LANES, :], preferred_element_type=F32)
    merged = gates[:, :D_MODEL] * br_a + gates[:, D_MODEL:] * br_b
    h_ref[...] = x + jnp.dot(merged.astype(BF16), wmo_ref[...], preferred_element_type=F32)


def _merge(x2, oa, outs, lses, batch, seq, norm_mix, w_merge_gate, b_merge_gate, w_br_gla, w_br_dil, w_mix_out):
    tm = TM_MERGE
    ns = seq // tm
    weights = [norm_mix.reshape(1, D_MODEL), w_merge_gate.astype(BF16), b_merge_gate.reshape(1, 2 * D_MODEL),
               w_br_gla.astype(BF16), w_br_dil.astype(BF16), w_mix_out.astype(BF16)]

    def row(w):
        return pl.BlockSpec((tm, w), lambda b, s: (b * ns + s, 0))

    grouped = [pl.BlockSpec((None, dil, tm // dil, DIL_GW), lambda b, s: (b, 0, s, 0)) for _, dil in DIL_GROUPS]
    return pl.pallas_call(
        _merge_kernel, grid=(batch, ns),
        in_specs=[row(D_MODEL), row(GLA_V)] + grouped + grouped + [_resident(w.shape) for w in weights],
        out_specs=row(D_MODEL), out_shape=jax.ShapeDtypeStruct((batch * seq, D_MODEL), F32),
        scratch_shapes=[pltpu.VMEM((tm, LANES), F32)] * 4,
        compiler_params=_params(("parallel", "parallel")), name="merge",
    )(x2, oa, *outs, *lses, *weights)


def _memkv_kernel(mem_ref, nmem_ref, wkv_ref, kn_ref, k_ref, v_ref):
    mn = _rms(mem_ref[...], nmem_ref[...]).astype(BF16)
    kv = jnp.dot(mn, wkv_ref[...], preferred_element_type=F32)
    kn = kn_ref[...]
    for h in range(X_HEADS):
        cs = slice(h * X_HEAD_DIM, (h + 1) * X_HEAD_DIM)
        k_ref[:, cs] = _rms(kv[:, cs], kn).astype(BF16)
    v_ref[...] = kv[:, D_MODEL:].astype(BF16)


def _mem_kv(mem, norm_mem, w_xkv, x_k_norm):
    batch, n_mem, _ = mem.shape
    blk = pl.BlockSpec((None, n_mem, D_MODEL), lambda b: (b, 0, 0))
    weights = [norm_mem.reshape(1, D_MODEL), w_xkv.astype(BF16), x_k_norm.reshape(1, X_HEAD_DIM)]
    out = jax.ShapeDtypeStruct((batch, n_mem, D_MODEL), BF16)
    return pl.pallas_call(
        _memkv_kernel, grid=(batch,),
        in_specs=[blk] + [_resident(w.shape) for w in weights],
        out_specs=[blk, blk], out_shape=[out, out],
        compiler_params=_params(("parallel",)), name="mem_kv",
    )(mem, *weights)


def _xattn_kernel(h_ref, k_ref, v_ref, nx_ref, wq_ref, qn_ref, wo_ref, o_ref):
    h = h_ref[...]
    xn = _rms(h, nx_ref[...]).astype(BF16)
    q = jnp.dot(xn, wq_ref[...], preferred_element_type=F32)
    qn = qn_ref[...]
    outs = []
    for hd in range(X_HEADS):
        cs = slice(hd * X_HEAD_DIM, (hd + 1) * X_HEAD_DIM)
        qh = _rms(q[:, cs], qn).astype(BF16)
        s = lax.dot_general(qh, k_ref[:, cs], (((1,), (1,)), ((), ())), preferred_element_type=F32)
        s = s * (X_HEAD_DIM ** -0.5)
        e = jnp.exp(s - jnp.max(s, axis=-1, keepdims=True))
        l = jnp.sum(e, axis=-1, keepdims=True)
        outs.append((jnp.dot(e.astype(BF16), v_ref[:, cs], preferred_element_type=F32) / l).astype(BF16))
    o = jnp.concatenate(outs, axis=-1)
    o_ref[...] = h + jnp.dot(o, wo_ref[...], preferred_element_type=F32)


def _cross_attention(h1, kmem, vmem, norm_x, w_xq, x_q_norm, w_xo, batch, seq):
    tm = TM_XATTN
    ns = seq // tm
    n_mem = kmem.shape[1]
    weights_a = [norm_x.reshape(1, D_MODEL), w_xq.astype(BF16), x_q_norm.reshape(1, X_HEAD_DIM), w_xo.astype(BF16)]
    row = pl.BlockSpec((tm, D_MODEL), lambda i: (i, 0))
    memblk = pl.BlockSpec((None, n_mem, D_MODEL), lambda i: (i // ns, 0, 0))
    return pl.pallas_call(
        _xattn_kernel, grid=(batch * ns,),
        in_specs=[row, memblk, memblk] + [_resident(w.shape) for w in weights_a],
        out_specs=row, out_shape=jax.ShapeDtypeStruct(h1.shape, F32),
        compiler_params=_params(("parallel",)), name="cross_attention",
    )(h1, kmem, vmem, *weights_a)


def _ffn_kernel(h_ref, nf_ref, wa_ref, wu_ref, wc_ref, bc_ref, wd_ref, o_ref, abuf_ref):
    tm = h_ref.shape[0]
    pad = 8

    @pl.when(pl.program_id(1) == 0)
    def _():
        abuf_ref[tm:tm + pad, :] = jnp.zeros((pad, D_FF), F32)

    h = h_ref[...]
    xn = _rms(h, nf_ref[...]).astype(BF16)
    acc = jnp.zeros((tm, D_MODEL), F32)
    for j in range(D_FF // FF_CHUNK):
        cs = slice(j * FF_CHUNK, (j + 1) * FF_CHUNK)
        a = jnp.dot(xn, wa_ref[:, cs], preferred_element_type=F32)
        u = jnp.dot(xn, wu_ref[:, cs], preferred_element_type=F32)
        abuf_ref[0:pad, cs] = abuf_ref[tm:tm + pad, cs]
        abuf_ref[pad:pad + tm, cs] = a
        conv = bc_ref[:, cs] + a * wc_ref[2:3, cs]
        conv = conv + abuf_ref[pad - 1:pad - 1 + tm, cs] * wc_ref[1:2, cs]
        conv = conv + abuf_ref[pad - 2:pad - 2 + tm, cs] * wc_ref[0:1, cs]
        gelu = 0.5 * conv * (1.0 + lax.erf(conv * (2.0 ** -0.5)))
        acc = acc + jnp.dot((gelu * u).astype(BF16), wd_ref[cs, :], preferred_element_type=F32)
    o_ref[...] = h + acc


def _ffn(h2, norm_ffn, w_ffn_up, w_ffn_conv, b_ffn_conv, w_ffn_down, batch, seq):
    tm = TM_FFN
    ns = seq // tm
    weights = [norm_ffn.reshape(1, D_MODEL), w_ffn_up[:, :D_FF].astype(BF16), w_ffn_up[:, D_FF:].astype(BF16),
               w_ffn_conv, b_ffn_conv.reshape(1, D_FF), w_ffn_down.astype(BF16)]
    row = pl.BlockSpec((tm, D_MODEL), lambda b, s: (b * ns + s, 0))
    return pl.pallas_call(
        _ffn_kernel, grid=(batch, ns),
        in_specs=[row] + [_resident(w.shape) for w in weights],
        out_specs=row, out_shape=jax.ShapeDtypeStruct(h2.shape, F32),
        scratch_shapes=[pltpu.VMEM((tm + 8, D_FF), F32)],
        compiler_params=_params(("parallel", "arbitrary")), name="conv_glu_ffn",
    )(h2, *weights)


def _layer(h, rope, mem, batch, seq, norm_mix, w_in, w_gla_gate, b_gla_gate, gla_out_norm, dil_q_norm, dil_k_norm,
           w_br_gla, w_br_dil, w_merge_gate, b_merge_gate, w_mix_out, norm_x, norm_mem, w_xq, w_xkv, x_q_norm,
           x_k_norm, w_xo, norm_ffn, w_ffn_up, w_ffn_conv, b_ffn_conv, w_ffn_down):
    (qa, ka, va, ra, la), qs, ks, vs = _in_projection(h, rope, batch, seq, norm_mix, w_in, w_gla_gate, b_gla_gate,
                                                      dil_q_norm, dil_k_norm)
    oa = _gla(qa, ka, va, ra, la, gla_out_norm, batch, seq)
    outs, lses = zip(*(_dilated_group(q, k, v) for q, k, v in zip(qs, ks, vs)))
    h1 = _merge(h, oa, outs, lses, batch, seq, norm_mix, w_merge_gate, b_merge_gate, w_br_gla, w_br_dil, w_mix_out)
    kmem, vmem = _mem_kv(mem, norm_mem, w_xkv, x_k_norm)
    h2 = _cross_attention(h1, kmem, vmem, norm_x, w_xq, x_q_norm, w_xo, batch, seq)
    return _ffn(h2, norm_ffn, w_ffn_up, w_ffn_conv, b_ffn_conv, w_ffn_down, batch, seq)


def kernel(x, mem, positions, norm_mix, w_in, w_gla_gate, b_gla_gate, gla_out_norm, dil_q_norm, dil_k_norm, w_br_gla, w_br_dil, w_merge_gate, b_merge_gate, w_mix_out, norm_x, norm_mem, w_xq, w_xkv, x_q_norm, x_k_norm, w_xo, norm_ffn, w_ffn_up, w_ffn_conv, b_ffn_conv, w_ffn_down):
    batch, seq, _ = x.shape
    rope = _rope_tables(positions)
    h = x.reshape(batch * seq, D_MODEL)
    stacked = (norm_mix, w_in, w_gla_gate, b_gla_gate, gla_out_norm, dil_q_norm, dil_k_norm, w_br_gla, w_br_dil,
               w_merge_gate, b_merge_gate, w_mix_out, norm_x, norm_mem, w_xq, w_xkv, x_q_norm, x_k_norm, w_xo,
               norm_ffn, w_ffn_up, w_ffn_conv, b_ffn_conv, w_ffn_down)
    for l in range(norm_mix.shape[0]):
        h = _layer(h, rope, mem, batch, seq, *(p[l] for p in stacked))
    return h.reshape(batch, seq, D_MODEL)
```

```python
import functools

import jax
import jax.numpy as jnp
from jax import lax
from jax.experimental import pallas as pl
from jax.experimental.pallas import tpu as pltpu

F32 = jnp.float32
BF16 = jnp.bfloat16

D_MODEL = 1024
EPS = 1e-6
GLA_HEADS = 4
GLA_DK = 128
GLA_DV = 256
GLA_QK = GLA_HEADS * GLA_DK
GLA_V = GLA_HEADS * GLA_DV
GLA_GATE_RANK = 16
GLA_TAU = 16.0
GLA_CHUNK = 64
DIL_GROUPS = ((128, 1), (512, 4), (2048, 16))
DIL_HEADS = 8
HEAD_DIM = 64
DIL_GW = DIL_HEADS * HEAD_DIM
DIL_W = len(DIL_GROUPS) * DIL_GW
DIL_BLOCK = 128
ROT_DIM = HEAD_DIM // 4
ROPE_THETA = 500000.0
LOG2_E = 1.4426950408889634
DIL_Q_SCALE = HEAD_DIM ** -0.5 * LOG2_E
STAT_LANES = 16
X_HEADS = 4
X_HEAD_DIM = D_MODEL // X_HEADS
D_FF = 2816
CONV_W = 3
IN_SIZES = (GLA_QK, GLA_QK, GLA_V, GLA_V, GLA_GATE_RANK, DIL_W, DIL_W, DIL_W)

LANES = 128
VMEM_LIMIT = 56 * 1024 * 1024

TM_PROJ = 512
TS_GLA = 512
DIL_Q_BLOCKS = 2
TM_MERGE = 512
TM_XATTN = 512
TM_FFN = 1024
FF_CHUNK = 256
FFN_HALO = 8


def _params(sem):
    return pltpu.CompilerParams(dimension_semantics=sem, vmem_limit_bytes=VMEM_LIMIT)


def _resident(shape):
    nd = len(shape)
    return pl.BlockSpec(shape, lambda *_: (0,) * nd, pipeline_mode=pl.Buffered(1))


def _rms(x, gain):
    return x * lax.rsqrt(jnp.mean(x * x, axis=-1, keepdims=True) + EPS) * gain


def _rope_kernel(pos_ref, invf_ref, c_ref, sa_ref, sb_ref):
    ang = pos_ref[...].astype(F32) * invf_ref[...]
    lane = lax.broadcasted_iota(jnp.int32, ang.shape, 1) & (HEAD_DIM - 1)
    cos, sin = jnp.cos(ang), jnp.sin(ang)
    half = ROT_DIM // 2
    c_ref[...] = jnp.where(lane < ROT_DIM, cos, 1.0)
    sa_ref[...] = jnp.where(lane < half, -sin, 0.0)
    sb_ref[...] = jnp.where((lane >= half) & (lane < ROT_DIM), sin, 0.0)


def _rope_tables(positions):
    t = positions.size
    tm = 512
    inv_freq = ROPE_THETA ** (-jnp.arange(0, ROT_DIM, 2, dtype=F32) / ROT_DIM)
    lane = jnp.arange(LANES) % HEAD_DIM
    invf = jnp.where(lane < ROT_DIM, inv_freq[lane % (ROT_DIM // 2)], 0.0).reshape(1, LANES)
    out = jax.ShapeDtypeStruct((t, LANES), F32)
    row = pl.BlockSpec((tm, LANES), lambda i: (i, 0))
    return pl.pallas_call(
        _rope_kernel, grid=(t // tm,),
        in_specs=[pl.BlockSpec((tm, 1), lambda i: (i, 0)), pl.BlockSpec((1, LANES), lambda i: (0, 0))],
        out_specs=[row, row, row], out_shape=[out, out, out],
        compiler_params=_params(("parallel",)), name="rope_tables",
    )(positions.reshape(t, 1), invf)


def _store_residue_major(o_ref, col, val, scr_ref, dil):
    cs = slice(col, col + LANES)
    if dil == 1:
        o_ref[0, :, cs] = val.astype(BF16)
        return
    n = val.shape[0] // dil
    scr_ref[...] = val
    for r in range(dil):
        o_ref[r, :, cs] = scr_ref[pl.ds(r, n, stride=dil), :].astype(BF16)


def _inproj_kernel(x_ref, c_ref, sa_ref, sb_ref, nmix_ref, wqa_ref, wka_ref, wva_ref, wra_ref, wza_ref,
                   wqb_ref, wkb_ref, wvb_ref, wgg_ref, bgg_ref, qn_ref, kn_ref, bd_ref,
                   qa_ref, ka_ref, va_ref, ra_ref, la_ref, *rest):
    n_groups = len(DIL_GROUPS)
    q_refs, k_refs, v_refs = rest[:n_groups], rest[n_groups:2 * n_groups], rest[2 * n_groups:3 * n_groups]
    scratch = rest[3 * n_groups:]
    xn = _rms(x_ref[...], nmix_ref[...]).astype(BF16)

    def proj(w_ref):
        return jnp.dot(xn, w_ref[...], preferred_element_type=F32)

    qa_ref[...] = proj(wqa_ref).astype(BF16)
    ka_ref[...] = proj(wka_ref).astype(BF16)
    va_ref[...] = proj(wva_ref).astype(BF16)
    ra_ref[...] = jax.nn.silu(proj(wra_ref)).astype(BF16)
    za = proj(wza_ref).astype(BF16)
    gate = jnp.dot(za, wgg_ref[...], preferred_element_type=F32) + bgg_ref[...]
    la_ref[...] = jax.nn.log_sigmoid(gate) * (1.0 / GLA_TAU)

    cos, sa, sb = c_ref[...], sa_ref[...], sb_ref[...]
    bd = bd_ref[...]
    chunks_per_group = DIL_GW // LANES
    uses = [0]

    def emit(o_refs, j, val):
        gi = j // chunks_per_group
        scr = scratch[uses[0] % len(scratch)]
        uses[0] += 1
        _store_residue_major(o_refs[gi], (j % chunks_per_group) * LANES, val, scr, DIL_GROUPS[gi][1])

    def norm_rope(w_ref, gain_ref, o_refs):
        y = proj(w_ref)
        gain = gain_ref[...]
        for j in range(DIL_W // LANES):
            yc = y[:, j * LANES:(j + 1) * LANES]
            ss = jnp.dot((yc * yc).astype(BF16), bd, preferred_element_type=F32)
            yn = yc * lax.rsqrt(ss * (1.0 / HEAD_DIM) + EPS) * gain
            rot = yn * cos + pltpu.roll(yn, LANES - ROT_DIM // 2, axis=1) * sa + pltpu.roll(yn, ROT_DIM // 2, axis=1) * sb
            emit(o_refs, j, rot)

    norm_rope(wqb_ref, qn_ref, q_refs)
    norm_rope(wkb_ref, kn_ref, k_refs)
    yv = proj(wvb_ref)
    for j in range(DIL_W // LANES):
        emit(v_refs, j, yv[:, j * LANES:(j + 1) * LANES])


def _in_projection(x2, rope, batch, seq, norm_mix, w_in, w_gla_gate, b_gla_gate, dil_q_norm, dil_k_norm):
    tm = TM_PROJ
    ns = seq // tm
    offs = [0]
    for s in IN_SIZES:
        offs.append(offs[-1] + s)
    wqa, wka, wva, wra, wza, wqb, wkb, wvb = (w_in[:, offs[i]:offs[i + 1]].astype(BF16) for i in range(8))
    wza = jnp.pad(wza, ((0, 0), (0, LANES - GLA_GATE_RANK)))
    wgg = jnp.pad(w_gla_gate.astype(BF16), ((0, LANES - GLA_GATE_RANK), (0, 0)))
    lane = jnp.arange(LANES)
    bd = (lane[:, None] // HEAD_DIM == lane[None, :] // HEAD_DIM).astype(BF16)
    qn = jnp.tile(dil_q_norm * DIL_Q_SCALE, LANES // HEAD_DIM).reshape(1, LANES)
    kn = jnp.tile(dil_k_norm, LANES // HEAD_DIM).reshape(1, LANES)
    weights = [norm_mix.reshape(1, D_MODEL), wqa, wka, wva, wra, wza, wqb, wkb, wvb, wgg,
               b_gla_gate.reshape(1, GLA_QK), qn, kn, bd]

    def row(w):
        return pl.BlockSpec((tm, w), lambda b, s: (b * ns + s, 0))

    t = batch * seq
    flat_w = [(GLA_QK, BF16), (GLA_QK, BF16), (GLA_V, BF16), (GLA_V, BF16), (GLA_QK, F32)]
    out_specs = [row(w) for w, _ in flat_w]
    out_shape = [jax.ShapeDtypeStruct((t, w), dt) for w, dt in flat_w]
    for _ in range(3):
        for _, dil in DIL_GROUPS:
            out_specs.append(pl.BlockSpec((None, dil, tm // dil, DIL_GW), lambda b, s: (b, 0, s, 0)))
            out_shape.append(jax.ShapeDtypeStruct((batch, dil, seq // dil, DIL_GW), BF16))
    outs = pl.pallas_call(
        _inproj_kernel, grid=(batch, ns),
        in_specs=[row(D_MODEL), row(LANES), row(LANES), row(LANES)] + [_resident(w.shape) for w in weights],
        out_specs=out_specs, out_shape=out_shape,
        scratch_shapes=[pltpu.VMEM((tm, LANES), F32)] * 4,
        compiler_params=_params(("parallel", "parallel")), name="in_projection",
    )(x2, *rope, *weights)
    n = len(DIL_GROUPS)
    return outs[:5], outs[5:5 + n], outs[5 + n:5 + 2 * n], outs[5 + 2 * n:]


def _gla_level_refs(g):
    c = GLA_CHUNK
    refs = []
    for h in (32, 16, 8):
        parts = [jnp.broadcast_to(g[m:m + 1, :], (2 * h, g.shape[1])) for m in range(h, c, 2 * h)]
        refs.append(parts[0] if len(parts) == 1 else jnp.concatenate(parts, axis=0))
    g3 = g.reshape(c // 8, 8, g.shape[1])
    sub = lax.broadcasted_iota(jnp.int32, g3.shape, 1)

    def pick(s):
        return jnp.broadcast_to(g3[:, s:s + 1, :], g3.shape)

    refs.append(pick(4).reshape(g.shape))
    refs.append(jnp.where(sub < 4, pick(2), pick(6)).reshape(g.shape))
    r1 = jnp.where(sub < 2, pick(1), jnp.where(sub < 4, pick(3), jnp.where(sub < 6, pick(5), pick(7))))
    refs.append(r1.reshape(g.shape))
    return refs


def _gla_pair_masks():
    c = GLA_CHUNK
    ri = jnp.arange(c)[:, None]
    ci = jnp.arange(c)[None, :]
    slabs = [((ri >> (b + 1)) == (ci >> (b + 1))) & (((ri >> b) & 1) == 1) & (((ci >> b) & 1) == 0)
             for b in (5, 4, 3, 2, 1, 0)]
    slabs.append(ri == ci)
    return jnp.stack(slabs).astype(F32)


def _gla_kernel(qa_ref, ka_ref, va_ref, ra_ref, la_ref, gn_ref, pm_ref, o_ref, state_ref):
    c = GLA_CHUNK

    @pl.when(pl.program_id(1) == 0)
    def _():
        state_ref[...] = jnp.zeros_like(state_ref)

    ri = lax.broadcasted_iota(jnp.int32, (c, c), 0)
    ci = lax.broadcasted_iota(jnp.int32, (c, c), 1)
    tril = (ri >= ci).astype(BF16)
    row = lax.broadcasted_iota(jnp.int32, (c, GLA_DK), 0)
    upper = [((row >> b) & 1) == 1 for b in (5, 4, 3, 2, 1, 0)]
    gn = gn_ref[...]

    def chunk(ic, carry):
        rows = pl.ds(pl.multiple_of(ic * c, c), c)
        la = la_ref[rows, :]
        la_hi = la.astype(BF16)
        la_lo = (la - la_hi.astype(F32)).astype(BF16)
        g_all = (jnp.dot(tril, la_hi, preferred_element_type=F32) + jnp.dot(tril, la_lo, preferred_element_type=F32))
        heads = range(GLA_HEADS)
        kcols = [slice(h * GLA_DK, (h + 1) * GLA_DK) for h in heads]
        vcols = [slice(h * GLA_DV, (h + 1) * GLA_DV) for h in heads]
        nt = (((1,), (1,)), ((), ()))
        qs = [qa_ref[rows, kcols[h]].astype(F32) * (GLA_DK ** -0.5) for h in heads]
        ks_ = [ka_ref[rows, kcols[h]].astype(F32) for h in heads]
        vs_ = [va_ref[rows, vcols[h]] for h in heads]
        gs = [g_all[:, kcols[h]] for h in heads]
        sts = [state_ref[h] for h in heads]
        o_inter = [lax.dot_general((qs[h] * jnp.exp(gs[h])).astype(BF16), sts[h].astype(BF16), nt,
                                   preferred_element_type=F32) for h in heads]
        grams, diags = [], []
        for h in heads:
            zs = []
            for up, gm in zip(upper, _gla_level_refs(gs[h])):
                d = gs[h] - gm
                zs.append((jnp.where(up, qs[h], ks_[h]) * jnp.exp(jnp.where(up, d, -d))).astype(BF16))
            z = jnp.stack(zs)
            grams.append(jnp.einsum('lik,ljk->lij', z, z, preferred_element_type=F32))
            diags.append(lax.dot_general(qs[h].astype(BF16), ks_[h].astype(BF16), nt, preferred_element_type=F32))
        n_levels = len(upper)
        attn = []
        for h in heads:
            a = pm_ref[n_levels] * diags[h]
            for l in range(n_levels):
                a = a + pm_ref[l] * grams[h][l]
            attn.append(a.astype(BF16))
        outs = [o_inter[h] + jnp.dot(attn[h], vs_[h], preferred_element_type=F32) for h in heads]
        for h in heads:
            g_last = gs[h][c - 1:c, :]
            kd = (ks_[h] * jnp.exp(g_last - gs[h])).astype(BF16)
            state_ref[h] = sts[h] * jnp.exp(g_last) + lax.dot_general(
                vs_[h], kd, (((0,), (0,)), ((), ())), preferred_element_type=F32)
        for h in heads:
            o = _rms(outs[h], gn)
            o_ref[rows, vcols[h]] = (o * ra_ref[rows, vcols[h]].astype(F32)).astype(BF16)
        return carry

    lax.fori_loop(0, qa_ref.shape[0] // c, chunk, 0, unroll=2)


def _gla(qa, ka, va, ra, la, gla_out_norm, batch, seq):
    ts = TS_GLA
    ns = seq // ts

    def row(w):
        return pl.BlockSpec((ts, w), lambda b, s: (b * ns + s, 0))

    return pl.pallas_call(
        _gla_kernel, grid=(batch, ns),
        in_specs=[row(GLA_QK), row(GLA_QK), row(GLA_V), row(GLA_V), row(GLA_QK),
                  _resident((1, GLA_DV)), _resident((7, GLA_CHUNK, GLA_CHUNK))],
        out_specs=row(GLA_V),
        out_shape=jax.ShapeDtypeStruct((batch * seq, GLA_V), BF16),
        scratch_shapes=[pltpu.VMEM((GLA_HEADS, GLA_DV, GLA_DK), F32)],
        compiler_params=_params(("parallel", "arbitrary")), name="gla",
    )(qa, ka, va, ra, la, gla_out_norm.reshape(1, GLA_DV), _gla_pair_masks())


def _dil_kernel(q_ref, kp_ref, kc_ref, vp_ref, vc_ref, acc_ref, m_ref, l_ref):
    blk = DIL_BLOCK
    has_prev = pl.program_id(2) > 0
    qi = lax.broadcasted_iota(jnp.int32, (blk, 2 * blk), 0)
    kj = lax.broadcasted_iota(jnp.int32, (blk, 2 * blk), 1)
    dist = qi + blk - kj
    band = (dist >= 0) & (dist <= blk)
    band_first = band & ((kj >= blk) | has_prev)
    lane = lax.broadcasted_iota(jnp.int32, (blk, LANES), 1)
    stat_slot = lane // STAT_LANES
    nt = (((1,), (1,)), ((), ()))
    pairs = DIL_GW // LANES
    for sb in range(q_ref.shape[0] // blk):
        rows = slice(sb * blk, (sb + 1) * blk)
        before = slice((sb - 1) * blk, sb * blk)
        mask = band_first if sb == 0 else band
        scores, values = [], []
        for p in range(pairs):
            cs = slice(p * LANES, (p + 1) * LANES)
            q2 = q_ref[rows, cs]
            kp, vp = (kp_ref[:, cs], vp_ref[:, cs]) if sb == 0 else (kc_ref[before, cs], vc_ref[before, cs])
            kk = jnp.concatenate([kp, kc_ref[rows, cs]], axis=0)
            values.append(jnp.concatenate([vp, vc_ref[rows, cs]], axis=0))
            for hh in range(LANES // HEAD_DIM):
                mine = (lane < HEAD_DIM) if hh == 0 else (lane >= HEAD_DIM)
                qm = jnp.where(mine, q2, jnp.zeros_like(q2))
                scores.append(lax.dot_general(qm, kk, nt, preferred_element_type=F32))
        probs = []
        m_all = jnp.zeros((blk, LANES), F32)
        l_all = jnp.zeros((blk, LANES), F32)
        for idx, s in enumerate(scores):
            s = jnp.where(mask, s, -jnp.inf)
            m = jnp.max(s, axis=-1, keepdims=True)
            e = jnp.exp2(s - m)
            l = jnp.sum(e, axis=-1, keepdims=True)
            probs.append(e.astype(BF16))
            m_all = jnp.where(stat_slot == idx, m, m_all)
            l_all = jnp.where(stat_slot == idx, l, l_all)
        m_ref[rows, :] = m_all
        l_ref[rows, :] = l_all
        for p in range(pairs):
            acc0 = jnp.dot(probs[2 * p], values[p], preferred_element_type=F32)
            acc1 = jnp.dot(probs[2 * p + 1], values[p], preferred_element_type=F32)
            acc_ref[rows, p * LANES:(p + 1) * LANES] = jnp.where(lane < HEAD_DIM, acc0, acc1).astype(BF16)


def _dilated_group(q, k, v):
    batch, dil, length, _ = q.shape
    per_step = DIL_Q_BLOCKS
    steps = length // (per_step * DIL_BLOCK)

    def cur(w):
        return pl.BlockSpec((None, None, per_step * DIL_BLOCK, w), lambda b, r, n: (b, r, n, 0))

    prev = pl.BlockSpec((None, None, DIL_BLOCK, DIL_GW), lambda b, r, n: (b, r, jnp.maximum(per_step * n - 1, 0), 0))
    stat = jax.ShapeDtypeStruct((batch, dil, length, LANES), F32)
    return pl.pallas_call(
        _dil_kernel, grid=(batch, dil, steps),
        in_specs=[cur(DIL_GW), prev, cur(DIL_GW), prev, cur(DIL_GW)],
        out_specs=[cur(DIL_GW), cur(LANES), cur(LANES)],
        out_shape=[jax.ShapeDtypeStruct(q.shape, BF16), stat, stat],
        compiler_params=_params(("parallel", "parallel", "arbitrary")), name=f"dilated_attn_d{dil}",
    )(q, k, k, v, v)


def _load_token_major(src_ref, col, scr_ref, dil):
    cs = slice(col, col + LANES)
    if dil == 1:
        return src_ref[0, :, cs].astype(F32)
    n = src_ref.shape[1]
    for r in range(dil):
        scr_ref[pl.ds(r, n, stride=dil), :] = src_ref[r, :, cs].astype(F32)
    return scr_ref[...]


def _merge_kernel(x_ref, oa_ref, a0_ref, a1_ref, a2_ref, m0_ref, m1_ref, m2_ref, l0_ref, l1_ref, l2_ref,
                  nmix_ref, wmg_ref, bmg_ref, wbg_ref, wbd_ref, wmo_ref, expand_ref, h_ref, *scratch):
    x = x_ref[...]
    xn = _rms(x, nmix_ref[...]).astype(BF16)
    gates = jax.nn.sigmoid(jnp.dot(xn, wmg_ref[...], preferred_element_type=F32) + bmg_ref[...])
    br_a = jnp.dot(oa_ref[...], wbg_ref[...], preferred_element_type=F32)
    a_refs, m_refs, l_refs = (a0_ref, a1_ref, a2_ref), (m0_ref, m1_ref, m2_ref), (l0_ref, l1_ref, l2_ref)
    dils = [dil for _, dil in DIL_GROUPS]
    uses = [0]

    def token_major(ref, col, dil):
        uses[0] += 1
        return _load_token_major(ref, col, scratch[uses[0] % len(scratch)], dil)

    ms = [token_major(r, 0, d) for r, d in zip(m_refs, dils)]
    ls = [token_major(r, 0, d) for r, d in zip(l_refs, dils)]
    top = jnp.maximum(jnp.maximum(ms[0], ms[1]), ms[2])
    es = [jnp.exp2(m - top) for m in ms]
    inv = 1.0 / (es[0] * ls[0] + es[1] * ls[1] + es[2] * ls[2])
    wide = [jnp.dot((e * inv).astype(BF16), expand_ref[...], preferred_element_type=F32) for e in es]
    br_b = jnp.zeros_like(br_a)
    for j in range(DIL_GW // LANES):
        cs = slice(j * LANES, (j + 1) * LANES)
        ob = sum(w[:, cs] * token_major(r, j * LANES, d) for w, r, d in zip(wide, a_refs, dils))
        br_b = br_b + jnp.dot(ob.astype(BF16), wbd_ref[cs, :], preferred_element_type=F32)
    merged = gates[:, :D_MODEL] * br_a + gates[:, D_MODEL:] * br_b
    h_ref[...] = x + jnp.dot(merged.astype(BF16), wmo_ref[...], preferred_element_type=F32)


def _merge(x2, oa, accs, ms, ls, batch, seq, norm_mix, w_merge_gate, b_merge_gate, w_br_gla, w_br_dil, w_mix_out):
    tm = TM_MERGE
    ns = seq // tm
    expand = (jnp.arange(LANES)[:, None] == STAT_LANES * (jnp.arange(DIL_GW)[None, :] // HEAD_DIM)).astype(BF16)
    weights = [norm_mix.reshape(1, D_MODEL), w_merge_gate.astype(BF16), b_merge_gate.reshape(1, 2 * D_MODEL),
               w_br_gla.astype(BF16), w_br_dil.astype(BF16), w_mix_out.astype(BF16), expand]

    def row(w):
        return pl.BlockSpec((tm, w), lambda b, s: (b * ns + s, 0))

    def grouped(w):
        return [pl.BlockSpec((None, dil, tm // dil, w), lambda b, s: (b, 0, s, 0)) for _, dil in DIL_GROUPS]

    return pl.pallas_call(
        _merge_kernel, grid=(batch, ns),
        in_specs=[row(D_MODEL), row(GLA_V)] + grouped(DIL_GW) + grouped(LANES) + grouped(LANES)
        + [_resident(w.shape) for w in weights],
        out_specs=row(D_MODEL), out_shape=jax.ShapeDtypeStruct((batch * seq, D_MODEL), F32),
        scratch_shapes=[pltpu.VMEM((tm, LANES), F32)] * 4,
        compiler_params=_params(("parallel", "parallel")), name="merge",
    )(x2, oa, *accs, *ms, *ls, *weights)


def _memkv_kernel(mem_ref, nmem_ref, wkv_ref, kn_ref, k_ref, v_ref):
    mn = _rms(mem_ref[...], nmem_ref[...]).astype(BF16)
    kv = jnp.dot(mn, wkv_ref[...], preferred_element_type=F32)
    kn = kn_ref[...]
    for h in range(X_HEADS):
        cs = slice(h * X_HEAD_DIM, (h + 1) * X_HEAD_DIM)
        k_ref[:, cs] = _rms(kv[:, cs], kn).astype(BF16)
    v_ref[...] = kv[:, D_MODEL:].astype(BF16)


def _mem_kv(mem, norm_mem, w_xkv, x_k_norm):
    batch, n_mem, _ = mem.shape
    blk = pl.BlockSpec((None, n_mem, D_MODEL), lambda b: (b, 0, 0))
    weights = [norm_mem.reshape(1, D_MODEL), w_xkv.astype(BF16), x_k_norm.reshape(1, X_HEAD_DIM)]
    out = jax.ShapeDtypeStruct((batch, n_mem, D_MODEL), BF16)
    return pl.pallas_call(
        _memkv_kernel, grid=(batch,),
        in_specs=[blk] + [_resident(w.shape) for w in weights],
        out_specs=[blk, blk], out_shape=[out, out],
        compiler_params=_params(("parallel",)), name="mem_kv",
    )(mem, *weights)


def _xattn_kernel(h_ref, k_ref, v_ref, nx_ref, wq_ref, qn_ref, wo_ref, o_ref):
    h = h_ref[...]
    xn = _rms(h, nx_ref[...]).astype(BF16)
    q = jnp.dot(xn, wq_ref[...], preferred_element_type=F32)
    qn = qn_ref[...]
    cols = [slice(hd * X_HEAD_DIM, (hd + 1) * X_HEAD_DIM) for hd in range(X_HEADS)]
    nt = (((1,), (1,)), ((), ()))
    scores = [lax.dot_general((_rms(q[:, cs], qn) * (X_HEAD_DIM ** -0.5 * LOG2_E)).astype(BF16), k_ref[:, cs], nt,
                              preferred_element_type=F32) for cs in cols]
    probs, inv = [], []
    for s in scores:
        e = jnp.exp2(s - jnp.max(s, axis=-1, keepdims=True))
        inv.append(1.0 / jnp.sum(e, axis=-1, keepdims=True))
        probs.append(e.astype(BF16))
    outs = [(jnp.dot(p, v_ref[:, cs], preferred_element_type=F32) * r).astype(BF16)
            for p, r, cs in zip(probs, inv, cols)]
    o = jnp.concatenate(outs, axis=-1)
    o_ref[...] = h + jnp.dot(o, wo_ref[...], preferred_element_type=F32)


def _cross_attention(h1, kmem, vmem, norm_x, w_xq, x_q_norm, w_xo, batch, seq):
    tm = TM_XATTN
    ns = seq // tm
    n_mem = kmem.shape[1]
    weights_a = [norm_x.reshape(1, D_MODEL), w_xq.astype(BF16), x_q_norm.reshape(1, X_HEAD_DIM), w_xo.astype(BF16)]
    row = pl.BlockSpec((tm, D_MODEL), lambda i: (i, 0))
    memblk = pl.BlockSpec((None, n_mem, D_MODEL), lambda i: (i // ns, 0, 0))
    return pl.pallas_call(
        _xattn_kernel, grid=(batch * ns,),
        in_specs=[row, memblk, memblk] + [_resident(w.shape) for w in weights_a],
        out_specs=row, out_shape=jax.ShapeDtypeStruct(h1.shape, F32),
        compiler_params=_params(("parallel",)), name="cross_attention",
    )(h1, kmem, vmem, *weights_a)


def _ffn_kernel(h_ref, nf_ref, wa_ref, wu_ref, wc_ref, bc_ref, wd_ref, o_ref, halo_ref, *stage_refs):
    tm = h_ref.shape[0]
    pad = FFN_HALO

    @pl.when(pl.program_id(1) == 0)
    def _():
        halo_ref[...] = jnp.zeros_like(halo_ref)

    h = h_ref[...]
    xn = _rms(h, nf_ref[...]).astype(BF16)
    acc = jnp.zeros((tm, D_MODEL), F32)
    for j in range(D_FF // FF_CHUNK):
        cs = slice(j * FF_CHUNK, (j + 1) * FF_CHUNK)
        stage = stage_refs[j % len(stage_refs)]
        a = jnp.dot(xn, wa_ref[:, cs], preferred_element_type=F32)
        u = jnp.dot(xn, wu_ref[:, cs], preferred_element_type=F32)
        stage[0:pad, :] = halo_ref[:, cs]
        stage[pad:pad + tm, :] = a
        halo_ref[:, cs] = a[tm - pad:tm, :]
        conv = bc_ref[:, cs] + a * wc_ref[2:3, cs]
        conv = conv + stage[pad - 1:pad - 1 + tm, :] * wc_ref[1:2, cs]
        conv = conv + stage[pad - 2:pad - 2 + tm, :] * wc_ref[0:1, cs]
        gelu = 0.5 * conv * (1.0 + lax.erf(conv * (2.0 ** -0.5)))
        acc = acc + jnp.dot((gelu * u).astype(BF16), wd_ref[cs, :], preferred_element_type=F32)
    o_ref[...] = h + acc


def _ffn(h2, norm_ffn, w_ffn_up, w_ffn_conv, b_ffn_conv, w_ffn_down, batch, seq):
    tm = TM_FFN
    ns = seq // tm
    weights = [norm_ffn.reshape(1, D_MODEL), w_ffn_up[:, :D_FF].astype(BF16), w_ffn_up[:, D_FF:].astype(BF16),
               w_ffn_conv, b_ffn_conv.reshape(1, D_FF), w_ffn_down.astype(BF16)]
    row = pl.BlockSpec((tm, D_MODEL), lambda b, s: (b * ns + s, 0))
    return pl.pallas_call(
        _ffn_kernel, grid=(batch, ns),
        in_specs=[row] + [_resident(w.shape) for w in weights],
        out_specs=row, out_shape=jax.ShapeDtypeStruct(h2.shape, F32),
        scratch_shapes=[pltpu.VMEM((FFN_HALO, D_FF), F32)] + [pltpu.VMEM((tm + FFN_HALO, FF_CHUNK), F32)] * 2,
        compiler_params=_params(("parallel", "arbitrary")), name="conv_glu_ffn",
    )(h2, *weights)


def _layer(h, rope, mem, batch, seq, norm_mix, w_in, w_gla_gate, b_gla_gate, gla_out_norm, dil_q_norm, dil_k_norm,
           w_br_gla, w_br_dil, w_merge_gate, b_merge_gate, w_mix_out, norm_x, norm_mem, w_xq, w_xkv, x_q_norm,
           x_k_norm, w_xo, norm_ffn, w_ffn_up, w_ffn_conv, b_ffn_conv, w_ffn_down):
    (qa, ka, va, ra, la), qs, ks, vs = _in_projection(h, rope, batch, seq, norm_mix, w_in, w_gla_gate, b_gla_gate,
                                                      dil_q_norm, dil_k_norm)
    oa = _gla(qa, ka, va, ra, la, gla_out_norm, batch, seq)
    accs, ms, ls = zip(*(_dilated_group(q, k, v) for q, k, v in zip(qs, ks, vs)))
    h1 = _merge(h, oa, accs, ms, ls, batch, seq, norm_mix, w_merge_gate, b_merge_gate, w_br_gla, w_br_dil, w_mix_out)
    kmem, vmem = _mem_kv(mem, norm_mem, w_xkv, x_k_norm)
    h2 = _cross_attention(h1, kmem, vmem, norm_x, w_xq, x_q_norm, w_xo, batch, seq)
    return _ffn(h2, norm_ffn, w_ffn_up, w_ffn_conv, b_ffn_conv, w_ffn_down, batch, seq)


def kernel(x, mem, positions, norm_mix, w_in, w_gla_gate, b_gla_gate, gla_out_norm, dil_q_norm, dil_k_norm, w_br_gla, w_br_dil, w_merge_gate, b_merge_gate, w_mix_out, norm_x, norm_mem, w_xq, w_xkv, x_q_norm, x_k_norm, w_xo, norm_ffn, w_ffn_up, w_ffn_conv, b_ffn_conv, w_ffn_down):
    batch, seq, _ = x.shape
    rope = _rope_tables(positions)
    h = x.reshape(batch * seq, D_MODEL)
    stacked = (norm_mix, w_in, w_gla_gate, b_gla_gate, gla_out_norm, dil_q_norm, dil_k_norm, w_br_gla, w_br_dil,
               w_merge_gate, b_merge_gate, w_mix_out, norm_x, norm_mem, w_xq, w_xkv, x_q_norm, x_k_norm, w_xo,
               norm_ffn, w_ffn_up, w_ffn_conv, b_ffn_conv, w_ffn_down)
    for l in range(norm_mix.shape[0]):
        h = _layer(h, rope, mem, batch, seq, *(p[l] for p in stacked))
    return h.reshape(batch, seq, D_MODEL)
```

```python
import functools

import jax
import jax.numpy as jnp
from jax import lax
from jax.experimental import pallas as pl
from jax.experimental.pallas import tpu as pltpu

F32 = jnp.float32
BF16 = jnp.bfloat16

D_MODEL = 1024
EPS = 1e-6
GLA_HEADS = 4
GLA_DK = 128
GLA_DV = 256
GLA_QK = GLA_HEADS * GLA_DK
GLA_V = GLA_HEADS * GLA_DV
GLA_GATE_RANK = 16
GLA_TAU = 16.0
GLA_CHUNK = 64
DIL_GROUPS = ((128, 1), (512, 4), (2048, 16))
DIL_HEADS = 8
HEAD_DIM = 64
DIL_GW = DIL_HEADS * HEAD_DIM
DIL_W = len(DIL_GROUPS) * DIL_GW
DIL_BLOCK = 128
ROT_DIM = HEAD_DIM // 4
ROPE_THETA = 500000.0
LOG2_E = 1.4426950408889634
DIL_Q_SCALE = HEAD_DIM ** -0.5 * LOG2_E
STAT_LANES = 16
X_HEADS = 4
X_HEAD_DIM = D_MODEL // X_HEADS
D_FF = 2816
CONV_W = 3
IN_SIZES = (GLA_QK, GLA_QK, GLA_V, GLA_V, GLA_GATE_RANK, DIL_W, DIL_W, DIL_W)

LANES = 128
VMEM_LIMIT = 56 * 1024 * 1024

TM_PROJ = 512
TS_GLA = 512
DIL_Q_BLOCKS = 4
TM_MERGE = 512
TM_XATTN = 512
TM_FFN = 1024
FF_CHUNK = 256
FFN_HALO = 8


def _params(sem):
    return pltpu.CompilerParams(dimension_semantics=sem, vmem_limit_bytes=VMEM_LIMIT)


def _resident(shape):
    nd = len(shape)
    return pl.BlockSpec(shape, lambda *_: (0,) * nd, pipeline_mode=pl.Buffered(1))


def _rms(x, gain):
    return x * lax.rsqrt(jnp.mean(x * x, axis=-1, keepdims=True) + EPS) * gain


def _store_residue_major(o_ref, col, val, scr_ref, dil):
    cs = slice(col, col + LANES)
    if dil == 1:
        o_ref[0, :, cs] = val.astype(BF16)
        return
    n = val.shape[0] // dil
    scr_ref[...] = val
    for r in range(dil):
        o_ref[r, :, cs] = scr_ref[pl.ds(r, n, stride=dil), :].astype(BF16)


def _inproj_kernel(x_ref, pos_ref, invf_ref, place_ref, nmix_ref, wqa_ref, wka_ref, wva_ref, wra_ref, wza_ref,
                   wqb_ref, wkb_ref, wvb_ref, wgg_ref, bgg_ref, qn_ref, kn_ref, bd_ref,
                   qa_ref, ka_ref, va_ref, ra_ref, la_ref, *rest):
    n_groups = len(DIL_GROUPS)
    q_refs, k_refs, v_refs = rest[:n_groups], rest[n_groups:2 * n_groups], rest[2 * n_groups:3 * n_groups]
    scratch = rest[3 * n_groups:]
    xn = _rms(x_ref[...], nmix_ref[...]).astype(BF16)

    def proj(w_ref):
        return jnp.dot(xn, w_ref[...], preferred_element_type=F32)

    ang = invf_ref[...] * pos_ref[...].astype(F32)
    terms = []
    for t in (jnp.cos(ang), jnp.sin(ang)):
        hi = t.astype(BF16)
        r1 = t - hi.astype(F32)
        mid = r1.astype(BF16)
        terms += [hi, mid, (r1 - mid.astype(F32)).astype(BF16)]
    tables = lax.dot_general(jnp.concatenate(terms, axis=0), place_ref[...], (((0,), (0,)), ((), ())),
                             preferred_element_type=F32)
    rot_lane = lax.broadcasted_iota(jnp.int32, (1, LANES), 1) & (HEAD_DIM - 1)
    cos = tables[:, :LANES] + jnp.where(rot_lane < ROT_DIM, 0.0, 1.0)
    sa, sb = tables[:, LANES:2 * LANES], tables[:, 2 * LANES:]
    bd = bd_ref[...]
    chunks_per_group = DIL_GW // LANES
    uses = [0]

    def emit(o_refs, j, val):
        gi = j // chunks_per_group
        scr = scratch[uses[0] % len(scratch)]
        uses[0] += 1
        _store_residue_major(o_refs[gi], (j % chunks_per_group) * LANES, val, scr, DIL_GROUPS[gi][1])

    def norm_rope(w_ref, gain_ref, o_refs):
        y = proj(w_ref)
        gain = gain_ref[...]
        for j in range(DIL_W // LANES):
            yc = y[:, j * LANES:(j + 1) * LANES]
            ss = jnp.dot((yc * yc).astype(BF16), bd, preferred_element_type=F32)
            yn = yc * lax.rsqrt(ss * (1.0 / HEAD_DIM) + EPS) * gain
            rot = yn * cos + pltpu.roll(yn, LANES - ROT_DIM // 2, axis=1) * sa + pltpu.roll(yn, ROT_DIM // 2, axis=1) * sb
            emit(o_refs, j, rot)

    norm_rope(wqb_ref, qn_ref, q_refs)
    norm_rope(wkb_ref, kn_ref, k_refs)
    yv = proj(wvb_ref)
    for j in range(DIL_W // LANES):
        emit(v_refs, j, yv[:, j * LANES:(j + 1) * LANES])

    za = proj(wza_ref).astype(BF16)
    gate = jnp.dot(za, wgg_ref[...], preferred_element_type=F32) + bgg_ref[...]
    la_ref[...] = jax.nn.log_sigmoid(gate) * (LOG2_E / GLA_TAU)
    ra_ref[...] = jax.nn.silu(proj(wra_ref)).astype(BF16)
    qa_ref[...] = proj(wqa_ref).astype(BF16)
    ka_ref[...] = proj(wka_ref).astype(BF16)
    va_ref[...] = proj(wva_ref).astype(BF16)


def _in_projection(x2, positions, batch, seq, norm_mix, w_in, w_gla_gate, b_gla_gate, dil_q_norm, dil_k_norm):
    tm = TM_PROJ
    ns = seq // tm
    offs = [0]
    for s in IN_SIZES:
        offs.append(offs[-1] + s)
    wqa, wka, wva, wra, wza, wqb, wkb, wvb = (w_in[:, offs[i]:offs[i + 1]].astype(BF16) for i in range(8))
    wza = jnp.pad(wza, ((0, 0), (0, LANES - GLA_GATE_RANK)))
    wgg = jnp.pad(w_gla_gate.astype(BF16), ((0, LANES - GLA_GATE_RANK), (0, 0)))
    lane = jnp.arange(LANES)
    bd = (lane[:, None] // HEAD_DIM == lane[None, :] // HEAD_DIM).astype(BF16)
    qn = jnp.tile(dil_q_norm * DIL_Q_SCALE, LANES // HEAD_DIM).reshape(1, LANES)
    kn = jnp.tile(dil_k_norm, LANES // HEAD_DIM).reshape(1, LANES)
    half = ROT_DIM // 2
    invf = (ROPE_THETA ** (-jnp.arange(0, ROT_DIM, 2, dtype=F32) / ROT_DIM)).reshape(half, 1)
    rot_lane = jnp.arange(LANES) % HEAD_DIM
    freq = jnp.arange(half)[:, None]
    cos_place = ((rot_lane[None, :] % half == freq) & (rot_lane[None, :] < ROT_DIM)).astype(F32)
    sa_place = -((rot_lane[None, :] == freq)).astype(F32)
    sb_place = ((rot_lane[None, :] == freq + half)).astype(F32)
    zero = jnp.zeros_like(cos_place)
    place = jnp.concatenate([jnp.tile(jnp.concatenate([cos_place, zero, zero], axis=1), (3, 1)),
                             jnp.tile(jnp.concatenate([zero, sa_place, sb_place], axis=1), (3, 1))], axis=0).astype(BF16)
    weights = [invf, place, norm_mix.reshape(1, D_MODEL), wqa, wka, wva, wra, wza, wqb, wkb, wvb, wgg,
               b_gla_gate.reshape(1, GLA_QK), qn, kn, bd]

    def row(w):
        return pl.BlockSpec((tm, w), lambda b, s: (b * ns + s, 0))

    t = batch * seq
    flat_w = [(GLA_QK, BF16), (GLA_QK, BF16), (GLA_V, BF16), (GLA_V, BF16), (GLA_QK, F32)]
    out_specs = [row(w) for w, _ in flat_w]
    out_shape = [jax.ShapeDtypeStruct((t, w), dt) for w, dt in flat_w]
    for _ in range(3):
        for _, dil in DIL_GROUPS:
            out_specs.append(pl.BlockSpec((None, dil, tm // dil, DIL_GW), lambda b, s: (b, 0, s, 0)))
            out_shape.append(jax.ShapeDtypeStruct((batch, dil, seq // dil, DIL_GW), BF16))
    outs = pl.pallas_call(
        _inproj_kernel, grid=(batch, ns),
        in_specs=[row(D_MODEL), pl.BlockSpec((None, 1, tm), lambda b, s: (b * ns + s, 0, 0))]
        + [_resident(w.shape) for w in weights],
        out_specs=out_specs, out_shape=out_shape,
        scratch_shapes=[pltpu.VMEM((tm, LANES), F32)] * 4,
        compiler_params=_params(("parallel", "parallel")), name="in_projection",
    )(x2, positions.reshape(batch * ns, 1, tm), *weights)
    n = len(DIL_GROUPS)
    return outs[:5], outs[5:5 + n], outs[5 + n:5 + 2 * n], outs[5 + 2 * n:]


def _gla_level_refs(g):
    c = GLA_CHUNK
    refs = []
    for h in (32, 16, 8):
        parts = [jnp.broadcast_to(g[m:m + 1, :], (2 * h, g.shape[1])) for m in range(h, c, 2 * h)]
        refs.append(parts[0] if len(parts) == 1 else jnp.concatenate(parts, axis=0))
    g3 = g.reshape(c // 8, 8, g.shape[1])
    sub = lax.broadcasted_iota(jnp.int32, g3.shape, 1)

    def pick(s):
        return jnp.broadcast_to(g3[:, s:s + 1, :], g3.shape)

    refs.append(pick(4).reshape(g.shape))
    refs.append(jnp.where(sub < 4, pick(2), pick(6)).reshape(g.shape))
    r1 = jnp.where(sub < 2, pick(1), jnp.where(sub < 4, pick(3), jnp.where(sub < 6, pick(5), pick(7))))
    refs.append(r1.reshape(g.shape))
    return refs


def _gla_pair_masks():
    c = GLA_CHUNK
    ri = jnp.arange(c)[:, None]
    ci = jnp.arange(c)[None, :]
    slabs = [((ri >> (b + 1)) == (ci >> (b + 1))) & (((ri >> b) & 1) == 1) & (((ci >> b) & 1) == 0)
             for b in (5, 4, 3, 2, 1, 0)]
    slabs.append(ri == ci)
    return jnp.stack(slabs).astype(F32)


def _gla_kernel(qa_ref, ka_ref, va_ref, ra_ref, la_ref, gn_ref, pm_ref, o_ref, state_ref):
    c = GLA_CHUNK

    @pl.when(pl.program_id(1) == 0)
    def _():
        state_ref[...] = jnp.zeros_like(state_ref)

    ri = lax.broadcasted_iota(jnp.int32, (c, c), 0)
    ci = lax.broadcasted_iota(jnp.int32, (c, c), 1)
    tril = (ri >= ci).astype(BF16)
    row = lax.broadcasted_iota(jnp.int32, (c, GLA_DK), 0)
    upper = [((row >> b) & 1) == 1 for b in (5, 4, 3, 2, 1, 0)]
    gn = gn_ref[...]

    def chunk(ic, carry):
        rows = pl.ds(pl.multiple_of(ic * c, c), c)
        la = la_ref[rows, :]
        la_hi = la.astype(BF16)
        la_lo = (la - la_hi.astype(F32)).astype(BF16)
        g_all = (jnp.dot(tril, la_hi, preferred_element_type=F32) + jnp.dot(tril, la_lo, preferred_element_type=F32))
        heads = range(GLA_HEADS)
        kcols = [slice(h * GLA_DK, (h + 1) * GLA_DK) for h in heads]
        vcols = [slice(h * GLA_DV, (h + 1) * GLA_DV) for h in heads]
        nt = (((1,), (1,)), ((), ()))
        qs = [qa_ref[rows, kcols[h]].astype(F32) * (GLA_DK ** -0.5) for h in heads]
        ks_ = [ka_ref[rows, kcols[h]].astype(F32) for h in heads]
        vs_ = [va_ref[rows, vcols[h]] for h in heads]
        gs = [g_all[:, kcols[h]] for h in heads]
        sts = [state_ref[h] for h in heads]
        o_inter = [lax.dot_general((qs[h] * jnp.exp2(gs[h])).astype(BF16), sts[h].astype(BF16), nt,
                                   preferred_element_type=F32) for h in heads]
        grams, diags = [], []
        for h in heads:
            zs = []
            for up, gm in zip(upper, _gla_level_refs(gs[h])):
                zs.append((jnp.where(up, qs[h], ks_[h]) * jnp.exp2(-jnp.abs(gs[h] - gm))).astype(BF16))
            z = jnp.stack(zs)
            grams.append(jnp.einsum('lik,ljk->lij', z, z, preferred_element_type=F32))
            diags.append(lax.dot_general(qs[h].astype(BF16), ks_[h].astype(BF16), nt, preferred_element_type=F32))
        n_levels = len(upper)
        attn = []
        for h in heads:
            a = pm_ref[n_levels] * diags[h]
            for l in range(n_levels):
                a = a + pm_ref[l] * grams[h][l]
            attn.append(a.astype(BF16))
        outs = [o_inter[h] + jnp.dot(attn[h], vs_[h], preferred_element_type=F32) for h in heads]
        for h in heads:
            g_last = gs[h][c - 1:c, :]
            kd = (ks_[h] * jnp.exp2(g_last - gs[h])).astype(BF16)
            state_ref[h] = sts[h] * jnp.exp2(g_last) + lax.dot_general(
                vs_[h], kd, (((0,), (0,)), ((), ())), preferred_element_type=F32)
        for h in heads:
            o = _rms(outs[h], gn)
            o_ref[rows, vcols[h]] = (o * ra_ref[rows, vcols[h]].astype(F32)).astype(BF16)
        return carry

    lax.fori_loop(0, qa_ref.shape[0] // c, chunk, 0, unroll=2)


def _gla(qa, ka, va, ra, la, gla_out_norm, batch, seq):
    ts = TS_GLA
    ns = seq // ts

    def row(w):
        return pl.BlockSpec((ts, w), lambda b, s: (b * ns + s, 0))

    return pl.pallas_call(
        _gla_kernel, grid=(batch, ns),
        in_specs=[row(GLA_QK), row(GLA_QK), row(GLA_V), row(GLA_V), row(GLA_QK),
                  _resident((1, GLA_DV)), _resident((7, GLA_CHUNK, GLA_CHUNK))],
        out_specs=row(GLA_V),
        out_shape=jax.ShapeDtypeStruct((batch * seq, GLA_V), BF16),
        scratch_shapes=[pltpu.VMEM((GLA_HEADS, GLA_DV, GLA_DK), F32)],
        compiler_params=_params(("parallel", "arbitrary")), name="gla",
    )(qa, ka, va, ra, la, gla_out_norm.reshape(1, GLA_DV), _gla_pair_masks())


def _dil_kernel(q_ref, kp_ref, kc_ref, vp_ref, vc_ref, acc_ref, m_ref, l_ref):
    blk = DIL_BLOCK
    has_prev = pl.program_id(2) > 0
    qi = lax.broadcasted_iota(jnp.int32, (blk, 2 * blk), 0)
    kj = lax.broadcasted_iota(jnp.int32, (blk, 2 * blk), 1)
    dist = qi + blk - kj
    band = (dist >= 0) & (dist <= blk)
    band_first = band & ((kj >= blk) | has_prev)
    lane = lax.broadcasted_iota(jnp.int32, (blk, LANES), 1)
    stat_slot = lane // STAT_LANES
    nt = (((1,), (1,)), ((), ()))
    pairs = DIL_GW // LANES
    for sb in range(q_ref.shape[0] // blk):
        rows = slice(sb * blk, (sb + 1) * blk)
        before = slice((sb - 1) * blk, sb * blk)
        mask = band_first if sb == 0 else band
        scores, values = [], []
        for p in range(pairs):
            cs = slice(p * LANES, (p + 1) * LANES)
            q2 = q_ref[rows, cs]
            kp, vp = (kp_ref[:, cs], vp_ref[:, cs]) if sb == 0 else (kc_ref[before, cs], vc_ref[before, cs])
            kk = jnp.concatenate([kp, kc_ref[rows, cs]], axis=0)
            values.append(jnp.concatenate([vp, vc_ref[rows, cs]], axis=0))
            for hh in range(LANES // HEAD_DIM):
                mine = (lane < HEAD_DIM) if hh == 0 else (lane >= HEAD_DIM)
                qm = jnp.where(mine, q2, jnp.zeros_like(q2))
                scores.append(lax.dot_general(qm, kk, nt, preferred_element_type=F32))
        probs = []
        m_all = jnp.zeros((blk, LANES), F32)
        l_all = jnp.zeros((blk, LANES), F32)
        for idx, s in enumerate(scores):
            s = jnp.where(mask, s, -jnp.inf)
            m = jnp.max(s, axis=-1, keepdims=True)
            e = jnp.exp2(s - m)
            l = jnp.sum(e, axis=-1, keepdims=True)
            probs.append(e.astype(BF16))
            m_all = jnp.where(stat_slot == idx, m, m_all)
            l_all = jnp.where(stat_slot == idx, l, l_all)
        m_ref[rows, :] = m_all
        l_ref[rows, :] = l_all
        for p in range(pairs):
            acc0 = jnp.dot(probs[2 * p], values[p], preferred_element_type=F32)
            acc1 = jnp.dot(probs[2 * p + 1], values[p], preferred_element_type=F32)
            acc_ref[rows, p * LANES:(p + 1) * LANES] = jnp.where(lane < HEAD_DIM, acc0, acc1).astype(BF16)


def _dilated_group(q, k, v):
    batch, dil, length, _ = q.shape
    per_step = min(DIL_Q_BLOCKS, length // DIL_BLOCK)
    steps = length // (per_step * DIL_BLOCK)

    def cur(w):
        return pl.BlockSpec((None, None, per_step * DIL_BLOCK, w), lambda b, r, n: (b, r, n, 0))

    prev = pl.BlockSpec((None, None, DIL_BLOCK, DIL_GW), lambda b, r, n: (b, r, jnp.maximum(per_step * n - 1, 0), 0))
    stat = jax.ShapeDtypeStruct((batch, dil, length, LANES), F32)
    return pl.pallas_call(
        _dil_kernel, grid=(batch, dil, steps),
        in_specs=[cur(DIL_GW), prev, cur(DIL_GW), prev, cur(DIL_GW)],
        out_specs=[cur(DIL_GW), cur(LANES), cur(LANES)],
        out_shape=[jax.ShapeDtypeStruct(q.shape, BF16), stat, stat],
        compiler_params=_params(("parallel", "parallel", "arbitrary")), name=f"dilated_attn_d{dil}",
    )(q, k, k, v, v)


def _load_token_major(src_ref, col, scr_ref, dil):
    cs = slice(col, col + LANES)
    if dil == 1:
        return src_ref[0, :, cs].astype(F32)
    n = src_ref.shape[1]
    for r in range(dil):
        scr_ref[pl.ds(r, n, stride=dil), :] = src_ref[r, :, cs].astype(F32)
    return scr_ref[...]


def _merge_kernel(x_ref, oa_ref, a0_ref, a1_ref, a2_ref, m0_ref, m1_ref, m2_ref, l0_ref, l1_ref, l2_ref,
                  nmix_ref, wmg_ref, bmg_ref, wbg_ref, wbd_ref, wmo_ref, expand_ref, h_ref, *scratch):
    x = x_ref[...]
    xn = _rms(x, nmix_ref[...]).astype(BF16)
    gates = jax.nn.sigmoid(jnp.dot(xn, wmg_ref[...], preferred_element_type=F32) + bmg_ref[...])
    br_a = jnp.dot(oa_ref[...], wbg_ref[...], preferred_element_type=F32)
    a_refs, m_refs, l_refs = (a0_ref, a1_ref, a2_ref), (m0_ref, m1_ref, m2_ref), (l0_ref, l1_ref, l2_ref)
    dils = [dil for _, dil in DIL_GROUPS]
    uses = [0]

    def token_major(ref, col, dil):
        uses[0] += 1
        return _load_token_major(ref, col, scratch[uses[0] % len(scratch)], dil)

    ms = [token_major(r, 0, d) for r, d in zip(m_refs, dils)]
    ls = [token_major(r, 0, d) for r, d in zip(l_refs, dils)]
    top = jnp.maximum(jnp.maximum(ms[0], ms[1]), ms[2])
    es = [jnp.exp2(m - top) for m in ms]
    inv = 1.0 / (es[0] * ls[0] + es[1] * ls[1] + es[2] * ls[2])
    wide = [jnp.dot((e * inv).astype(BF16), expand_ref[...], preferred_element_type=F32) for e in es]
    br_b = jnp.zeros_like(br_a)
    for j in range(DIL_GW // LANES):
        cs = slice(j * LANES, (j + 1) * LANES)
        ob = sum(w[:, cs] * token_major(r, j * LANES, d) for w, r, d in zip(wide, a_refs, dils))
        br_b = br_b + jnp.dot(ob.astype(BF16), wbd_ref[cs, :], preferred_element_type=F32)
    merged = gates[:, :D_MODEL] * br_a + gates[:, D_MODEL:] * br_b
    h_ref[...] = x + jnp.dot(merged.astype(BF16), wmo_ref[...], preferred_element_type=F32)


def _merge(x2, oa, accs, ms, ls, batch, seq, norm_mix, w_merge_gate, b_merge_gate, w_br_gla, w_br_dil, w_mix_out):
    tm = TM_MERGE
    ns = seq // tm
    expand = (jnp.arange(LANES)[:, None] == STAT_LANES * (jnp.arange(DIL_GW)[None, :] // HEAD_DIM)).astype(BF16)
    weights = [norm_mix.reshape(1, D_MODEL), w_merge_gate.astype(BF16), b_merge_gate.reshape(1, 2 * D_MODEL),
               w_br_gla.astype(BF16), w_br_dil.astype(BF16), w_mix_out.astype(BF16), expand]

    def row(w):
        return pl.BlockSpec((tm, w), lambda b, s: (b * ns + s, 0))

    def grouped(w):
        return [pl.BlockSpec((None, dil, tm // dil, w), lambda b, s: (b, 0, s, 0)) for _, dil in DIL_GROUPS]

    return pl.pallas_call(
        _merge_kernel, grid=(batch, ns),
        in_specs=[row(D_MODEL), row(GLA_V)] + grouped(DIL_GW) + grouped(LANES) + grouped(LANES)
        + [_resident(w.shape) for w in weights],
        out_specs=row(D_MODEL), out_shape=jax.ShapeDtypeStruct((batch * seq, D_MODEL), F32),
        scratch_shapes=[pltpu.VMEM((tm, LANES), F32)] * 4,
        compiler_params=_params(("parallel", "parallel")), name="merge",
    )(x2, oa, *accs, *ms, *ls, *weights)


def _memkv_kernel(mem_ref, nmem_ref, wkv_ref, kn_ref, k_ref, v_ref):
    mn = _rms(mem_ref[...], nmem_ref[...]).astype(BF16)
    kv = jnp.dot(mn, wkv_ref[...], preferred_element_type=F32)
    kn = kn_ref[...]
    for h in range(X_HEADS):
        cs = slice(h * X_HEAD_DIM, (h + 1) * X_HEAD_DIM)
        k_ref[:, cs] = _rms(kv[:, cs], kn).astype(BF16)
    v_ref[...] = kv[:, D_MODEL:].astype(BF16)


def _mem_kv(mem, norm_mem, w_xkv, x_k_norm):
    batch, n_mem, _ = mem.shape
    blk = pl.BlockSpec((None, n_mem, D_MODEL), lambda b: (b, 0, 0))
    weights = [norm_mem.reshape(1, D_MODEL), w_xkv.astype(BF16), x_k_norm.reshape(1, X_HEAD_DIM)]
    out = jax.ShapeDtypeStruct((batch, n_mem, D_MODEL), BF16)
    return pl.pallas_call(
        _memkv_kernel, grid=(batch,),
        in_specs=[blk] + [_resident(w.shape) for w in weights],
        out_specs=[blk, blk], out_shape=[out, out],
        compiler_params=_params(("parallel",)), name="mem_kv",
    )(mem, *weights)


def _xattn_kernel(h_ref, k_ref, v_ref, nx_ref, wq_ref, qn_ref, wo_ref, o_ref):
    h = h_ref[...]
    xn = _rms(h, nx_ref[...]).astype(BF16)
    q = jnp.dot(xn, wq_ref[...], preferred_element_type=F32)
    qn = qn_ref[...]
    cols = [slice(hd * X_HEAD_DIM, (hd + 1) * X_HEAD_DIM) for hd in range(X_HEADS)]
    nt = (((1,), (1,)), ((), ()))
    scores = [lax.dot_general((_rms(q[:, cs], qn) * (X_HEAD_DIM ** -0.5 * LOG2_E)).astype(BF16), k_ref[:, cs], nt,
                              preferred_element_type=F32) for cs in cols]
    probs, inv = [], []
    for s in scores:
        e = jnp.exp2(s - jnp.max(s, axis=-1, keepdims=True))
        inv.append(1.0 / jnp.sum(e, axis=-1, keepdims=True))
        probs.append(e.astype(BF16))
    outs = [(jnp.dot(p, v_ref[:, cs], preferred_element_type=F32) * r).astype(BF16)
            for p, r, cs in zip(probs, inv, cols)]
    o = jnp.concatenate(outs, axis=-1)
    o_ref[...] = h + jnp.dot(o, wo_ref[...], preferred_element_type=F32)


def _cross_attention(h1, kmem, vmem, norm_x, w_xq, x_q_norm, w_xo, batch, seq):
    tm = TM_XATTN
    ns = seq // tm
    n_mem = kmem.shape[1]
    weights_a = [norm_x.reshape(1, D_MODEL), w_xq.astype(BF16), x_q_norm.reshape(1, X_HEAD_DIM), w_xo.astype(BF16)]
    row = pl.BlockSpec((tm, D_MODEL), lambda i: (i, 0))
    memblk = pl.BlockSpec((None, n_mem, D_MODEL), lambda i: (i // ns, 0, 0))
    return pl.pallas_call(
        _xattn_kernel, grid=(batch * ns,),
        in_specs=[row, memblk, memblk] + [_resident(w.shape) for w in weights_a],
        out_specs=row, out_shape=jax.ShapeDtypeStruct(h1.shape, F32),
        compiler_params=_params(("parallel",)), name="cross_attention",
    )(h1, kmem, vmem, *weights_a)


def _ffn_kernel(h_ref, nf_ref, wa_ref, wu_ref, wc_ref, bc_ref, wd_ref, o_ref, halo_ref, *stage_refs):
    tm = h_ref.shape[0]
    pad = FFN_HALO

    @pl.when(pl.program_id(1) == 0)
    def _():
        halo_ref[...] = jnp.zeros_like(halo_ref)

    h = h_ref[...]
    xn = _rms(h, nf_ref[...]).astype(BF16)
    acc = jnp.zeros((tm, D_MODEL), F32)
    for j in range(D_FF // FF_CHUNK):
        cs = slice(j * FF_CHUNK, (j + 1) * FF_CHUNK)
        stage = stage_refs[j % len(stage_refs)]
        a = jnp.dot(xn, wa_ref[:, cs], preferred_element_type=F32)
        u = jnp.dot(xn, wu_ref[:, cs], preferred_element_type=F32)
        stage[0:pad, :] = halo_ref[:, cs]
        stage[pad:pad + tm, :] = a
        halo_ref[:, cs] = a[tm - pad:tm, :]
        conv = bc_ref[:, cs] + a * wc_ref[2:3, cs]
        conv = conv + stage[pad - 1:pad - 1 + tm, :] * wc_ref[1:2, cs]
        conv = conv + stage[pad - 2:pad - 2 + tm, :] * wc_ref[0:1, cs]
        gelu = 0.5 * conv * (1.0 + lax.erf(conv * (2.0 ** -0.5)))
        acc = acc + jnp.dot((gelu * u).astype(BF16), wd_ref[cs, :], preferred_element_type=F32)
    o_ref[...] = h + acc


def _ffn(h2, norm_ffn, w_ffn_up, w_ffn_conv, b_ffn_conv, w_ffn_down, batch, seq):
    tm = TM_FFN
    ns = seq // tm
    weights = [norm_ffn.reshape(1, D_MODEL), w_ffn_up[:, :D_FF].astype(BF16), w_ffn_up[:, D_FF:].astype(BF16),
               w_ffn_conv, b_ffn_conv.reshape(1, D_FF), w_ffn_down.astype(BF16)]
    row = pl.BlockSpec((tm, D_MODEL), lambda b, s: (b * ns + s, 0))
    return pl.pallas_call(
        _ffn_kernel, grid=(batch, ns),
        in_specs=[row] + [_resident(w.shape) for w in weights],
        out_specs=row, out_shape=jax.ShapeDtypeStruct(h2.shape, F32),
        scratch_shapes=[pltpu.VMEM((FFN_HALO, D_FF), F32)] + [pltpu.VMEM((tm + FFN_HALO, FF_CHUNK), F32)] * 2,
        compiler_params=_params(("parallel", "arbitrary")), name="conv_glu_ffn",
    )(h2, *weights)


def _layer(h, positions, mem, batch, seq, norm_mix, w_in, w_gla_gate, b_gla_gate, gla_out_norm, dil_q_norm, dil_k_norm,
           w_br_gla, w_br_dil, w_merge_gate, b_merge_gate, w_mix_out, norm_x, norm_mem, w_xq, w_xkv, x_q_norm,
           x_k_norm, w_xo, norm_ffn, w_ffn_up, w_ffn_conv, b_ffn_conv, w_ffn_down):
    (qa, ka, va, ra, la), qs, ks, vs = _in_projection(h, positions, batch, seq, norm_mix, w_in, w_gla_gate, b_gla_gate,
                                                      dil_q_norm, dil_k_norm)
    oa = _gla(qa, ka, va, ra, la, gla_out_norm, batch, seq)
    accs, ms, ls = zip(*(_dilated_group(q, k, v) for q, k, v in zip(qs, ks, vs)))
    h1 = _merge(h, oa, accs, ms, ls, batch, seq, norm_mix, w_merge_gate, b_merge_gate, w_br_gla, w_br_dil, w_mix_out)
    kmem, vmem = _mem_kv(mem, norm_mem, w_xkv, x_k_norm)
    h2 = _cross_attention(h1, kmem, vmem, norm_x, w_xq, x_q_norm, w_xo, batch, seq)
    return _ffn(h2, norm_ffn, w_ffn_up, w_ffn_conv, b_ffn_conv, w_ffn_down, batch, seq)


def kernel(x, mem, positions, norm_mix, w_in, w_gla_gate, b_gla_gate, gla_out_norm, dil_q_norm, dil_k_norm, w_br_gla, w_br_dil, w_merge_gate, b_merge_gate, w_mix_out, norm_x, norm_mem, w_xq, w_xkv, x_q_norm, x_k_norm, w_xo, norm_ffn, w_ffn_up, w_ffn_conv, b_ffn_conv, w_ffn_down):
    batch, seq, _ = x.shape
    h = x.reshape(batch * seq, D_MODEL)
    stacked = (norm_mix, w_in, w_gla_gate, b_gla_gate, gla_out_norm, dil_q_norm, dil_k_norm, w_br_gla, w_br_dil,
               w_merge_gate, b_merge_gate, w_mix_out, norm_x, norm_mem, w_xq, w_xkv, x_q_norm, x_k_norm, w_xo,
               norm_ffn, w_ffn_up, w_ffn_conv, b_ffn_conv, w_ffn_down)
    for l in range(norm_mix.shape[0]):
        h = _layer(h, positions, mem, batch, seq, *(p[l] for p in stacked))
    return h.reshape(batch, seq, D_MODEL)
```

```python
import functools

import jax
import jax.numpy as jnp
from jax import lax
from jax.experimental import pallas as pl
from jax.experimental.pallas import tpu as pltpu

F32 = jnp.float32
BF16 = jnp.bfloat16

D_MODEL = 1024
EPS = 1e-6
GLA_HEADS = 4
GLA_DK = 128
GLA_DV = 256
GLA_QK = GLA_HEADS * GLA_DK
GLA_V = GLA_HEADS * GLA_DV
GLA_GATE_RANK = 16
GLA_TAU = 16.0
GLA_CHUNK = 64
DIL_GROUPS = ((128, 1), (512, 4), (2048, 16))
DIL_HEADS = 8
HEAD_DIM = 64
DIL_GW = DIL_HEADS * HEAD_DIM
DIL_W = len(DIL_GROUPS) * DIL_GW
DIL_BLOCK = 128
ROT_DIM = HEAD_DIM // 4
ROPE_THETA = 500000.0
LOG2_E = 1.4426950408889634
DIL_Q_SCALE = HEAD_DIM ** -0.5 * LOG2_E
STAT_LANES = 16
X_HEADS = 4
X_HEAD_DIM = D_MODEL // X_HEADS
D_FF = 2816
CONV_W = 3
IN_SIZES = (GLA_QK, GLA_QK, GLA_V, GLA_V, GLA_GATE_RANK, DIL_W, DIL_W, DIL_W)

LANES = 128
VMEM_LIMIT = 56 * 1024 * 1024

TM_PROJ = 512
TS_GLA = 512
DIL_Q_BLOCKS = 4
TM_MERGE = 512
TM_XATTN = 512
TM_FFN = 512
FF_CHUNK = D_FF
FFN_HALO = 8


def _params(sem):
    return pltpu.CompilerParams(dimension_semantics=sem, vmem_limit_bytes=VMEM_LIMIT)


def _resident(shape):
    nd = len(shape)
    return pl.BlockSpec(shape, lambda *_: (0,) * nd, pipeline_mode=pl.Buffered(1))


def _rms(x, gain):
    return x * lax.rsqrt(jnp.mean(x * x, axis=-1, keepdims=True) + EPS) * gain


def _store_residue_major(o_ref, col, val, scr_ref, dil):
    cs = slice(col, col + LANES)
    if dil == 1:
        o_ref[0, :, cs] = val.astype(BF16)
        return
    n = val.shape[0] // dil
    scr_ref[...] = val
    for r in range(dil):
        o_ref[r, :, cs] = scr_ref[pl.ds(r, n, stride=dil), :].astype(BF16)


def _inproj_kernel(x_ref, pos_ref, invf_ref, place_ref, nmix_ref, wqa_ref, wka_ref, wva_ref, wra_ref, wza_ref,
                   wqb_ref, wkb_ref, wvb_ref, wgg_ref, bgg_ref, qn_ref, kn_ref, bd_ref,
                   qa_ref, ka_ref, va_ref, ra_ref, la_ref, *rest):
    n_groups = len(DIL_GROUPS)
    q_refs, k_refs, v_refs = rest[:n_groups], rest[n_groups:2 * n_groups], rest[2 * n_groups:3 * n_groups]
    scratch = rest[3 * n_groups:]
    xn = _rms(x_ref[...], nmix_ref[...]).astype(BF16)

    def proj(w_ref):
        return jnp.dot(xn, w_ref[...], preferred_element_type=F32)

    ang = invf_ref[...] * pos_ref[...].astype(F32)
    terms = []
    for t in (jnp.cos(ang), jnp.sin(ang)):
        hi = t.astype(BF16)
        r1 = t - hi.astype(F32)
        mid = r1.astype(BF16)
        terms += [hi, mid, (r1 - mid.astype(F32)).astype(BF16)]
    tables = lax.dot_general(jnp.concatenate(terms, axis=0), place_ref[...], (((0,), (0,)), ((), ())),
                             preferred_element_type=F32)
    rot_lane = lax.broadcasted_iota(jnp.int32, (1, LANES), 1) & (HEAD_DIM - 1)
    cos = tables[:, :LANES] + jnp.where(rot_lane < ROT_DIM, 0.0, 1.0)
    sa, sb = tables[:, LANES:2 * LANES], tables[:, 2 * LANES:]
    bd = bd_ref[...]
    chunks_per_group = DIL_GW // LANES
    uses = [0]

    def emit(o_refs, j, val):
        gi = j // chunks_per_group
        scr = scratch[uses[0] % len(scratch)]
        uses[0] += 1
        _store_residue_major(o_refs[gi], (j % chunks_per_group) * LANES, val, scr, DIL_GROUPS[gi][1])

    def norm_rope(w_ref, gain_ref, o_refs):
        y = proj(w_ref)
        gain = gain_ref[...]
        for j in range(DIL_W // LANES):
            yc = y[:, j * LANES:(j + 1) * LANES]
            ss = jnp.dot((yc * yc).astype(BF16), bd, preferred_element_type=F32)
            yn = yc * lax.rsqrt(ss * (1.0 / HEAD_DIM) + EPS) * gain
            rot = yn * cos + pltpu.roll(yn, LANES - ROT_DIM // 2, axis=1) * sa + pltpu.roll(yn, ROT_DIM // 2, axis=1) * sb
            emit(o_refs, j, rot)

    norm_rope(wqb_ref, qn_ref, q_refs)
    norm_rope(wkb_ref, kn_ref, k_refs)
    yv = proj(wvb_ref)
    for j in range(DIL_W // LANES):
        emit(v_refs, j, yv[:, j * LANES:(j + 1) * LANES])

    za = proj(wza_ref).astype(BF16)
    gate = jnp.dot(za, wgg_ref[...], preferred_element_type=F32) + bgg_ref[...]
    la_ref[...] = jax.nn.log_sigmoid(gate) * (LOG2_E / GLA_TAU)
    ra_ref[...] = jax.nn.silu(proj(wra_ref)).astype(BF16)
    qa_ref[...] = proj(wqa_ref).astype(BF16)
    ka_ref[...] = proj(wka_ref).astype(BF16)
    va_ref[...] = proj(wva_ref).astype(BF16)


def _in_projection(x2, positions, batch, seq, norm_mix, w_in, w_gla_gate, b_gla_gate, dil_q_norm, dil_k_norm):
    tm = TM_PROJ
    ns = seq // tm
    offs = [0]
    for s in IN_SIZES:
        offs.append(offs[-1] + s)
    wqa, wka, wva, wra, wza, wqb, wkb, wvb = (w_in[:, offs[i]:offs[i + 1]].astype(BF16) for i in range(8))
    wza = jnp.pad(wza, ((0, 0), (0, LANES - GLA_GATE_RANK)))
    wgg = jnp.pad(w_gla_gate.astype(BF16), ((0, LANES - GLA_GATE_RANK), (0, 0)))
    lane = jnp.arange(LANES)
    bd = (lane[:, None] // HEAD_DIM == lane[None, :] // HEAD_DIM).astype(BF16)
    qn = jnp.tile(dil_q_norm * DIL_Q_SCALE, LANES // HEAD_DIM).reshape(1, LANES)
    kn = jnp.tile(dil_k_norm, LANES // HEAD_DIM).reshape(1, LANES)
    half = ROT_DIM // 2
    invf = (ROPE_THETA ** (-jnp.arange(0, ROT_DIM, 2, dtype=F32) / ROT_DIM)).reshape(half, 1)
    rot_lane = jnp.arange(LANES) % HEAD_DIM
    freq = jnp.arange(half)[:, None]
    cos_place = ((rot_lane[None, :] % half == freq) & (rot_lane[None, :] < ROT_DIM)).astype(F32)
    sa_place = -((rot_lane[None, :] == freq)).astype(F32)
    sb_place = ((rot_lane[None, :] == freq + half)).astype(F32)
    zero = jnp.zeros_like(cos_place)
    place = jnp.concatenate([jnp.tile(jnp.concatenate([cos_place, zero, zero], axis=1), (3, 1)),
                             jnp.tile(jnp.concatenate([zero, sa_place, sb_place], axis=1), (3, 1))], axis=0).astype(BF16)
    weights = [invf, place, norm_mix.reshape(1, D_MODEL), wqa, wka, wva, wra, wza, wqb, wkb, wvb, wgg,
               b_gla_gate.reshape(1, GLA_QK), qn, kn, bd]

    def row(w):
        return pl.BlockSpec((tm, w), lambda b, s: (b * ns + s, 0))

    t = batch * seq
    flat_w = [(GLA_QK, BF16), (GLA_QK, BF16), (GLA_V, BF16), (GLA_V, BF16), (GLA_QK, F32)]
    out_specs = [row(w) for w, _ in flat_w]
    out_shape = [jax.ShapeDtypeStruct((t, w), dt) for w, dt in flat_w]
    for _ in range(3):
        for _, dil in DIL_GROUPS:
            out_specs.append(pl.BlockSpec((None, dil, tm // dil, DIL_GW), lambda b, s: (b, 0, s, 0)))
            out_shape.append(jax.ShapeDtypeStruct((batch, dil, seq // dil, DIL_GW), BF16))
    outs = pl.pallas_call(
        _inproj_kernel, grid=(batch, ns),
        in_specs=[row(D_MODEL), pl.BlockSpec((None, 1, tm), lambda b, s: (b * ns + s, 0, 0))]
        + [_resident(w.shape) for w in weights],
        out_specs=out_specs, out_shape=out_shape,
        scratch_shapes=[pltpu.VMEM((tm, LANES), F32)] * 4,
        compiler_params=_params(("parallel", "parallel")), name="in_projection",
    )(x2, positions.reshape(batch * ns, 1, tm), *weights)
    n = len(DIL_GROUPS)
    return outs[:5], outs[5:5 + n], outs[5 + n:5 + 2 * n], outs[5 + 2 * n:]


def _gla_level_refs(g):
    c = GLA_CHUNK
    refs = []
    for h in (32, 16, 8):
        parts = [jnp.broadcast_to(g[m:m + 1, :], (2 * h, g.shape[1])) for m in range(h, c, 2 * h)]
        refs.append(parts[0] if len(parts) == 1 else jnp.concatenate(parts, axis=0))
    g3 = g.reshape(c // 8, 8, g.shape[1])
    sub = lax.broadcasted_iota(jnp.int32, g3.shape, 1)

    def pick(s):
        return jnp.broadcast_to(g3[:, s:s + 1, :], g3.shape)

    refs.append(pick(4).reshape(g.shape))
    refs.append(jnp.where(sub < 4, pick(2), pick(6)).reshape(g.shape))
    r1 = jnp.where(sub < 2, pick(1), jnp.where(sub < 4, pick(3), jnp.where(sub < 6, pick(5), pick(7))))
    refs.append(r1.reshape(g.shape))
    return refs


def _gla_pair_masks():
    c = GLA_CHUNK
    ri = jnp.arange(c)[:, None]
    ci = jnp.arange(c)[None, :]
    slabs = [((ri >> (b + 1)) == (ci >> (b + 1))) & (((ri >> b) & 1) == 1) & (((ci >> b) & 1) == 0)
             for b in (5, 4, 3, 2, 1, 0)]
    slabs.append(ri == ci)
    return jnp.stack(slabs).astype(F32)


def _gla_kernel(qa_ref, ka_ref, va_ref, ra_ref, la_ref, gn_ref, pm_ref, o_ref, state_ref):
    c = GLA_CHUNK

    @pl.when(pl.program_id(1) == 0)
    def _():
        state_ref[...] = jnp.zeros_like(state_ref)

    ri = lax.broadcasted_iota(jnp.int32, (c, c), 0)
    ci = lax.broadcasted_iota(jnp.int32, (c, c), 1)
    tril = (ri >= ci).astype(BF16)
    row = lax.broadcasted_iota(jnp.int32, (c, GLA_DK), 0)
    upper = [((row >> b) & 1) == 1 for b in (5, 4, 3, 2, 1, 0)]
    gn = gn_ref[...]

    def chunk(ic, carry):
        rows = pl.ds(pl.multiple_of(ic * c, c), c)
        la = la_ref[rows, :]
        la_hi = la.astype(BF16)
        la_lo = (la - la_hi.astype(F32)).astype(BF16)
        g_all = (jnp.dot(tril, la_hi, preferred_element_type=F32) + jnp.dot(tril, la_lo, preferred_element_type=F32))
        heads = range(GLA_HEADS)
        kcols = [slice(h * GLA_DK, (h + 1) * GLA_DK) for h in heads]
        vcols = [slice(h * GLA_DV, (h + 1) * GLA_DV) for h in heads]
        nt = (((1,), (1,)), ((), ()))
        qs = [qa_ref[rows, kcols[h]].astype(F32) * (GLA_DK ** -0.5) for h in heads]
        ks_ = [ka_ref[rows, kcols[h]].astype(F32) for h in heads]
        vs_ = [va_ref[rows, vcols[h]] for h in heads]
        gs = [g_all[:, kcols[h]] for h in heads]
        sts = [state_ref[h] for h in heads]
        o_inter = [lax.dot_general((qs[h] * jnp.exp2(gs[h])).astype(BF16), sts[h].astype(BF16), nt,
                                   preferred_element_type=F32) for h in heads]
        grams, diags = [], []
        for h in heads:
            zs = []
            for up, gm in zip(upper, _gla_level_refs(gs[h])):
                zs.append((jnp.where(up, qs[h], ks_[h]) * jnp.exp2(-jnp.abs(gs[h] - gm))).astype(BF16))
            z = jnp.stack(zs)
            grams.append(jnp.einsum('lik,ljk->lij', z, z, preferred_element_type=F32))
            diags.append(lax.dot_general(qs[h].astype(BF16), ks_[h].astype(BF16), nt, preferred_element_type=F32))
        n_levels = len(upper)
        attn = []
        for h in heads:
            a = pm_ref[n_levels] * diags[h]
            for l in range(n_levels):
                a = a + pm_ref[l] * grams[h][l]
            attn.append(a.astype(BF16))
        outs = [o_inter[h] + jnp.dot(attn[h], vs_[h], preferred_element_type=F32) for h in heads]
        for h in heads:
            g_last = gs[h][c - 1:c, :]
            kd = (ks_[h] * jnp.exp2(g_last - gs[h])).astype(BF16)
            state_ref[h] = sts[h] * jnp.exp2(g_last) + lax.dot_general(
                vs_[h], kd, (((0,), (0,)), ((), ())), preferred_element_type=F32)
        for h in heads:
            o = _rms(outs[h], gn)
            o_ref[rows, vcols[h]] = (o * ra_ref[rows, vcols[h]].astype(F32)).astype(BF16)
        return carry

    lax.fori_loop(0, qa_ref.shape[0] // c, chunk, 0, unroll=2)


def _gla(qa, ka, va, ra, la, gla_out_norm, batch, seq):
    ts = TS_GLA
    ns = seq // ts

    def row(w):
        return pl.BlockSpec((ts, w), lambda b, s: (b * ns + s, 0))

    return pl.pallas_call(
        _gla_kernel, grid=(batch, ns),
        in_specs=[row(GLA_QK), row(GLA_QK), row(GLA_V), row(GLA_V), row(GLA_QK),
                  _resident((1, GLA_DV)), _resident((7, GLA_CHUNK, GLA_CHUNK))],
        out_specs=row(GLA_V),
        out_shape=jax.ShapeDtypeStruct((batch * seq, GLA_V), BF16),
        scratch_shapes=[pltpu.VMEM((GLA_HEADS, GLA_DV, GLA_DK), F32)],
        compiler_params=_params(("parallel", "arbitrary")), name="gla",
    )(qa, ka, va, ra, la, gla_out_norm.reshape(1, GLA_DV), _gla_pair_masks())


def _dil_kernel(q_ref, kp_ref, kc_ref, vp_ref, vc_ref, acc_ref, m_ref, l_ref):
    blk = DIL_BLOCK
    has_prev = pl.program_id(2) > 0
    qi = lax.broadcasted_iota(jnp.int32, (blk, 2 * blk), 0)
    kj = lax.broadcasted_iota(jnp.int32, (blk, 2 * blk), 1)
    dist = qi + blk - kj
    band = (dist >= 0) & (dist <= blk)
    band_first = band & ((kj >= blk) | has_prev)
    lane = lax.broadcasted_iota(jnp.int32, (blk, LANES), 1)
    stat_slot = lane // STAT_LANES
    nt = (((1,), (1,)), ((), ()))
    pairs = DIL_GW // LANES
    for res, sb in [(res, sb) for res in range(q_ref.shape[0]) for sb in range(q_ref.shape[1] // blk)]:
        rows = slice(sb * blk, (sb + 1) * blk)
        before = slice((sb - 1) * blk, sb * blk)
        mask = band_first if sb == 0 else band
        scores, values = [], []
        for p in range(pairs):
            cs = slice(p * LANES, (p + 1) * LANES)
            q2 = q_ref[res, rows, cs]
            kp, vp = ((kp_ref[res, :, cs], vp_ref[res, :, cs]) if sb == 0
                      else (kc_ref[res, before, cs], vc_ref[res, before, cs]))
            kk = jnp.concatenate([kp, kc_ref[res, rows, cs]], axis=0)
            values.append(jnp.concatenate([vp, vc_ref[res, rows, cs]], axis=0))
            for hh in range(LANES // HEAD_DIM):
                mine = (lane < HEAD_DIM) if hh == 0 else (lane >= HEAD_DIM)
                qm = jnp.where(mine, q2, jnp.zeros_like(q2))
                scores.append(lax.dot_general(qm, kk, nt, preferred_element_type=F32))
        probs = []
        m_all = jnp.zeros((blk, LANES), F32)
        l_all = jnp.zeros((blk, LANES), F32)
        for idx, s in enumerate(scores):
            s = jnp.where(mask, s, -jnp.inf)
            m = jnp.max(s, axis=-1, keepdims=True)
            e = jnp.exp2(s - m)
            l = jnp.sum(e, axis=-1, keepdims=True)
            probs.append(e.astype(BF16))
            m_all = jnp.where(stat_slot == idx, m, m_all)
            l_all = jnp.where(stat_slot == idx, l, l_all)
        m_ref[res, rows, :] = m_all
        l_ref[res, rows, :] = l_all
        for p in range(pairs):
            acc0 = jnp.dot(probs[2 * p], values[p], preferred_element_type=F32)
            acc1 = jnp.dot(probs[2 * p + 1], values[p], preferred_element_type=F32)
            acc_ref[res, rows, p * LANES:(p + 1) * LANES] = jnp.where(lane < HEAD_DIM, acc0, acc1).astype(BF16)


def _dilated_group(q, k, v):
    batch, dil, length, _ = q.shape
    per_step = min(DIL_Q_BLOCKS, length // DIL_BLOCK)
    steps = length // (per_step * DIL_BLOCK)
    residues = DIL_Q_BLOCKS // per_step

    def cur(w):
        return pl.BlockSpec((None, residues, per_step * DIL_BLOCK, w), lambda b, r, n: (b, r, n, 0))

    prev = pl.BlockSpec((None, residues, DIL_BLOCK, DIL_GW), lambda b, r, n: (b, r, jnp.maximum(per_step * n - 1, 0), 0))
    stat = jax.ShapeDtypeStruct((batch, dil, length, LANES), F32)
    return pl.pallas_call(
        _dil_kernel, grid=(batch, dil // residues, steps),
        in_specs=[cur(DIL_GW), prev, cur(DIL_GW), prev, cur(DIL_GW)],
        out_specs=[cur(DIL_GW), cur(LANES), cur(LANES)],
        out_shape=[jax.ShapeDtypeStruct(q.shape, BF16), stat, stat],
        compiler_params=_params(("parallel", "parallel", "arbitrary")), name=f"dilated_attn_d{dil}",
    )(q, k, k, v, v)


def _load_token_major(src_ref, col, scr_ref, dil):
    cs = slice(col, col + LANES)
    if dil == 1:
        return src_ref[0, :, cs].astype(F32)
    n = src_ref.shape[1]
    for r in range(dil):
        scr_ref[pl.ds(r, n, stride=dil), :] = src_ref[r, :, cs].astype(F32)
    return scr_ref[...]


def _merge_kernel(x_ref, oa_ref, a0_ref, a1_ref, a2_ref, m0_ref, m1_ref, m2_ref, l0_ref, l1_ref, l2_ref,
                  nmix_ref, wmg_ref, bmg_ref, wbg_ref, wbd_ref, wmo_ref, h_ref, *scratch):
    x = x_ref[...]
    xn = _rms(x, nmix_ref[...]).astype(BF16)
    gates = jax.nn.sigmoid(jnp.dot(xn, wmg_ref[...], preferred_element_type=F32) + bmg_ref[...])
    br_a = jnp.dot(oa_ref[...], wbg_ref[...], preferred_element_type=F32)
    a_refs, m_refs, l_refs = (a0_ref, a1_ref, a2_ref), (m0_ref, m1_ref, m2_ref), (l0_ref, l1_ref, l2_ref)
    dils = [dil for _, dil in DIL_GROUPS]
    uses = [0]

    def token_major(ref, col, dil):
        uses[0] += 1
        return _load_token_major(ref, col, scratch[uses[0] % len(scratch)], dil)

    ms = [token_major(r, 0, d) for r, d in zip(m_refs, dils)]
    ls = [token_major(r, 0, d) for r, d in zip(l_refs, dils)]
    top = jnp.maximum(jnp.maximum(ms[0], ms[1]), ms[2])
    es = [jnp.exp2(m - top) for m in ms]
    inv = 1.0 / (es[0] * ls[0] + es[1] * ls[1] + es[2] * ls[2])
    def widen(w):
        cols = [jnp.broadcast_to(w[:, STAT_LANES * hd:STAT_LANES * hd + 1], (w.shape[0], HEAD_DIM))
                for hd in range(DIL_HEADS)]
        return jnp.concatenate(cols, axis=1)

    wide = [widen(e * inv) for e in es]
    br_b = jnp.zeros_like(br_a)
    for j in range(DIL_GW // LANES):
        cs = slice(j * LANES, (j + 1) * LANES)
        ob = sum(w[:, cs] * token_major(r, j * LANES, d) for w, r, d in zip(wide, a_refs, dils))
        br_b = br_b + jnp.dot(ob.astype(BF16), wbd_ref[cs, :], preferred_element_type=F32)
    merged = gates[:, :D_MODEL] * br_a + gates[:, D_MODEL:] * br_b
    h_ref[...] = x + jnp.dot(merged.astype(BF16), wmo_ref[...], preferred_element_type=F32)


def _merge(x2, oa, accs, ms, ls, batch, seq, norm_mix, w_merge_gate, b_merge_gate, w_br_gla, w_br_dil, w_mix_out):
    tm = TM_MERGE
    ns = seq // tm
    weights = [norm_mix.reshape(1, D_MODEL), w_merge_gate.astype(BF16), b_merge_gate.reshape(1, 2 * D_MODEL),
               w_br_gla.astype(BF16), w_br_dil.astype(BF16), w_mix_out.astype(BF16)]

    def row(w):
        return pl.BlockSpec((tm, w), lambda b, s: (b * ns + s, 0))

    def grouped(w):
        return [pl.BlockSpec((None, dil, tm // dil, w), lambda b, s: (b, 0, s, 0)) for _, dil in DIL_GROUPS]

    return pl.pallas_call(
        _merge_kernel, grid=(batch, ns),
        in_specs=[row(D_MODEL), row(GLA_V)] + grouped(DIL_GW) + grouped(LANES) + grouped(LANES)
        + [_resident(w.shape) for w in weights],
        out_specs=row(D_MODEL), out_shape=jax.ShapeDtypeStruct((batch * seq, D_MODEL), F32),
        scratch_shapes=[pltpu.VMEM((tm, LANES), F32)] * 4,
        compiler_params=_params(("parallel", "parallel")), name="merge",
    )(x2, oa, *accs, *ms, *ls, *weights)


def _memkv_kernel(mem_ref, nmem_ref, wkv_ref, kn_ref, k_ref, v_ref):
    mn = _rms(mem_ref[...], nmem_ref[...]).astype(BF16)
    kv = jnp.dot(mn, wkv_ref[...], preferred_element_type=F32)
    kn = kn_ref[...]
    for h in range(X_HEADS):
        cs = slice(h * X_HEAD_DIM, (h + 1) * X_HEAD_DIM)
        k_ref[:, cs] = _rms(kv[:, cs], kn).astype(BF16)
    v_ref[...] = kv[:, D_MODEL:].astype(BF16)


def _mem_kv(mem, norm_mem, w_xkv, x_k_norm):
    batch, n_mem, _ = mem.shape
    blk = pl.BlockSpec((None, n_mem, D_MODEL), lambda b: (b, 0, 0))
    weights = [norm_mem.reshape(1, D_MODEL), w_xkv.astype(BF16), x_k_norm.reshape(1, X_HEAD_DIM)]
    out = jax.ShapeDtypeStruct((batch, n_mem, D_MODEL), BF16)
    return pl.pallas_call(
        _memkv_kernel, grid=(batch,),
        in_specs=[blk] + [_resident(w.shape) for w in weights],
        out_specs=[blk, blk], out_shape=[out, out],
        compiler_params=_params(("parallel",)), name="mem_kv",
    )(mem, *weights)


def _xattn_kernel(h_ref, k_ref, v_ref, nx_ref, wq_ref, qn_ref, wo_ref, o_ref):
    h = h_ref[...]
    xn = _rms(h, nx_ref[...]).astype(BF16)
    q = jnp.dot(xn, wq_ref[...], preferred_element_type=F32)
    qn = qn_ref[...]
    cols = [slice(hd * X_HEAD_DIM, (hd + 1) * X_HEAD_DIM) for hd in range(X_HEADS)]
    nt = (((1,), (1,)), ((), ()))
    scores = [lax.dot_general((_rms(q[:, cs], qn) * (X_HEAD_DIM ** -0.5 * LOG2_E)).astype(BF16), k_ref[:, cs], nt,
                              preferred_element_type=F32) for cs in cols]
    probs, inv = [], []
    for s in scores:
        e = jnp.exp2(s - jnp.max(s, axis=-1, keepdims=True))
        inv.append(1.0 / jnp.sum(e, axis=-1, keepdims=True))
        probs.append(e.astype(BF16))
    outs = [(jnp.dot(p, v_ref[:, cs], preferred_element_type=F32) * r).astype(BF16)
            for p, r, cs in zip(probs, inv, cols)]
    o = jnp.concatenate(outs, axis=-1)
    o_ref[...] = h + jnp.dot(o, wo_ref[...], preferred_element_type=F32)


def _cross_attention(h1, kmem, vmem, norm_x, w_xq, x_q_norm, w_xo, batch, seq):
    tm = TM_XATTN
    ns = seq // tm
    n_mem = kmem.shape[1]
    weights_a = [norm_x.reshape(1, D_MODEL), w_xq.astype(BF16), x_q_norm.reshape(1, X_HEAD_DIM), w_xo.astype(BF16)]
    row = pl.BlockSpec((tm, D_MODEL), lambda i: (i, 0))
    memblk = pl.BlockSpec((None, n_mem, D_MODEL), lambda i: (i // ns, 0, 0))
    return pl.pallas_call(
        _xattn_kernel, grid=(batch * ns,),
        in_specs=[row, memblk, memblk] + [_resident(w.shape) for w in weights_a],
        out_specs=row, out_shape=jax.ShapeDtypeStruct(h1.shape, F32),
        compiler_params=_params(("parallel",)), name="cross_attention",
    )(h1, kmem, vmem, *weights_a)


def _ffn_kernel(h_ref, nf_ref, wa_ref, wu_ref, wc_ref, bc_ref, wd_ref, o_ref, halo_ref, *stage_refs):
    tm = h_ref.shape[0]
    pad = FFN_HALO

    @pl.when(pl.program_id(1) == 0)
    def _():
        halo_ref[...] = jnp.zeros_like(halo_ref)

    h = h_ref[...]
    xn = _rms(h, nf_ref[...]).astype(BF16)
    acc = jnp.zeros((tm, D_MODEL), F32)
    for j in range(D_FF // FF_CHUNK):
        cs = slice(j * FF_CHUNK, (j + 1) * FF_CHUNK)
        stage = stage_refs[j % len(stage_refs)]
        a = jnp.dot(xn, wa_ref[:, cs], preferred_element_type=F32)
        u = jnp.dot(xn, wu_ref[:, cs], preferred_element_type=F32)
        stage[0:pad, :] = halo_ref[:, cs]
        stage[pad:pad + tm, :] = a
        halo_ref[:, cs] = a[tm - pad:tm, :]
        conv = bc_ref[:, cs] + a * wc_ref[2:3, cs]
        conv = conv + stage[pad - 1:pad - 1 + tm, :] * wc_ref[1:2, cs]
        conv = conv + stage[pad - 2:pad - 2 + tm, :] * wc_ref[0:1, cs]
        gelu = 0.5 * conv * (1.0 + lax.erf(conv * (2.0 ** -0.5)))
        acc = acc + jnp.dot((gelu * u).astype(BF16), wd_ref[cs, :], preferred_element_type=F32)
    o_ref[...] = h + acc


def _ffn(h2, norm_ffn, w_ffn_up, w_ffn_conv, b_ffn_conv, w_ffn_down, batch, seq):
    tm = TM_FFN
    ns = seq // tm
    weights = [norm_ffn.reshape(1, D_MODEL), w_ffn_up[:, :D_FF].astype(BF16), w_ffn_up[:, D_FF:].astype(BF16),
               w_ffn_conv, b_ffn_conv.reshape(1, D_FF), w_ffn_down.astype(BF16)]
    row = pl.BlockSpec((tm, D_MODEL), lambda b, s: (b * ns + s, 0))
    return pl.pallas_call(
        _ffn_kernel, grid=(batch, ns),
        in_specs=[row] + [_resident(w.shape) for w in weights],
        out_specs=row, out_shape=jax.ShapeDtypeStruct(h2.shape, F32),
        scratch_shapes=[pltpu.VMEM((FFN_HALO, D_FF), F32)]
        + [pltpu.VMEM((tm + FFN_HALO, FF_CHUNK), F32)] * min(2, D_FF // FF_CHUNK),
        compiler_params=_params(("parallel", "arbitrary")), name="conv_glu_ffn",
    )(h2, *weights)


def _layer(h, positions, mem, batch, seq, norm_mix, w_in, w_gla_gate, b_gla_gate, gla_out_norm, dil_q_norm, dil_k_norm,
           w_br_gla, w_br_dil, w_merge_gate, b_merge_gate, w_mix_out, norm_x, norm_mem, w_xq, w_xkv, x_q_norm,
           x_k_norm, w_xo, norm_ffn, w_ffn_up, w_ffn_conv, b_ffn_conv, w_ffn_down):
    (qa, ka, va, ra, la), qs, ks, vs = _in_projection(h, positions, batch, seq, norm_mix, w_in, w_gla_gate, b_gla_gate,
                                                      dil_q_norm, dil_k_norm)
    oa = _gla(qa, ka, va, ra, la, gla_out_norm, batch, seq)
    accs, ms, ls = zip(*(_dilated_group(q, k, v) for q, k, v in zip(qs, ks, vs)))
    h1 = _merge(h, oa, accs, ms, ls, batch, seq, norm_mix, w_merge_gate, b_merge_gate, w_br_gla, w_br_dil, w_mix_out)
    kmem, vmem = _mem_kv(mem, norm_mem, w_xkv, x_k_norm)
    h2 = _cross_attention(h1, kmem, vmem, norm_x, w_xq, x_q_norm, w_xo, batch, seq)
    return _ffn(h2, norm_ffn, w_ffn_up, w_ffn_conv, b_ffn_conv, w_ffn_down, batch, seq)


def kernel(x, mem, positions, norm_mix, w_in, w_gla_gate, b_gla_gate, gla_out_norm, dil_q_norm, dil_k_norm, w_br_gla, w_br_dil, w_merge_gate, b_merge_gate, w_mix_out, norm_x, norm_mem, w_xq, w_xkv, x_q_norm, x_k_norm, w_xo, norm_ffn, w_ffn_up, w_ffn_conv, b_ffn_conv, w_ffn_down):
    batch, seq, _ = x.shape
    h = x.reshape(batch * seq, D_MODEL)
    stacked = (norm_mix, w_in, w_gla_gate, b_gla_gate, gla_out_norm, dil_q_norm, dil_k_norm, w_br_gla, w_br_dil,
               w_merge_gate, b_merge_gate, w_mix_out, norm_x, norm_mem, w_xq, w_xkv, x_q_norm, x_k_norm, w_xo,
               norm_ffn, w_ffn_up, w_ffn_conv, b_ffn_conv, w_ffn_down)
    for l in range(norm_mix.shape[0]):
        h = _layer(h, positions, mem, batch, seq, *(p[l] for p in stacked))
    return h.reshape(batch, seq, D_MODEL)
```

```python
import functools

import jax
import jax.numpy as jnp
from jax import lax
from jax.experimental import pallas as pl
from jax.experimental.pallas import tpu as pltpu

F32 = jnp.float32
BF16 = jnp.bfloat16

D_MODEL = 1024
EPS = 1e-6
GLA_HEADS = 4
GLA_DK = 128
GLA_DV = 256
GLA_QK = GLA_HEADS * GLA_DK
GLA_V = GLA_HEADS * GLA_DV
GLA_GATE_RANK = 16
GLA_TAU = 16.0
GLA_CHUNK = 64
DIL_GROUPS = ((128, 1), (512, 4), (2048, 16))
DIL_HEADS = 8
HEAD_DIM = 64
DIL_GW = DIL_HEADS * HEAD_DIM
DIL_W = len(DIL_GROUPS) * DIL_GW
DIL_BLOCK = 128
ROT_DIM = HEAD_DIM // 4
ROPE_THETA = 500000.0
LOG2_E = 1.4426950408889634
DIL_Q_SCALE = HEAD_DIM ** -0.5 * LOG2_E
STAT_LANES = 16
X_HEADS = 4
X_HEAD_DIM = D_MODEL // X_HEADS
D_FF = 2816
CONV_W = 3
IN_SIZES = (GLA_QK, GLA_QK, GLA_V, GLA_V, GLA_GATE_RANK, DIL_W, DIL_W, DIL_W)

LANES = 128
VMEM_LIMIT = 56 * 1024 * 1024

TM_PROJ = 512
TS_GLA = 256
GLA_SEQS = 4
DIL_Q_BLOCKS = 4
TM_MERGE = 512
TM_XATTN = 512
TM_FFN = 512
FF_CHUNK = D_FF
FFN_HALO = 8


def _params(sem):
    return pltpu.CompilerParams(dimension_semantics=sem, vmem_limit_bytes=VMEM_LIMIT)


def _resident(shape):
    nd = len(shape)
    return pl.BlockSpec(shape, lambda *_: (0,) * nd, pipeline_mode=pl.Buffered(1))


def _rms(x, gain):
    return x * lax.rsqrt(jnp.mean(x * x, axis=-1, keepdims=True) + EPS) * gain


def _store_residue_major(o_ref, col, val, scr_ref, dil):
    cs = slice(col, col + LANES)
    if dil == 1:
        o_ref[0, :, cs] = val.astype(BF16)
        return
    n = val.shape[0] // dil
    scr_ref[...] = val
    for r in range(dil):
        o_ref[r, :, cs] = scr_ref[pl.ds(r, n, stride=dil), :].astype(BF16)


def _inproj_kernel(x_ref, pos_ref, invf_ref, place_ref, nmix_ref, wqa_ref, wka_ref, wva_ref, wra_ref, wza_ref,
                   wqb_ref, wkb_ref, wvb_ref, wgg_ref, bgg_ref, qn_ref, kn_ref, bd_ref,
                   qa_ref, ka_ref, va_ref, ra_ref, la_ref, *rest):
    n_groups = len(DIL_GROUPS)
    q_refs, k_refs, v_refs = rest[:n_groups], rest[n_groups:2 * n_groups], rest[2 * n_groups:3 * n_groups]
    scratch = rest[3 * n_groups:]
    xn = _rms(x_ref[...], nmix_ref[...]).astype(BF16)

    def proj(w_ref):
        return jnp.dot(xn, w_ref[...], preferred_element_type=F32)

    ang = invf_ref[...] * pos_ref[...].astype(F32)
    terms = []
    for t in (jnp.cos(ang), jnp.sin(ang)):
        hi = t.astype(BF16)
        r1 = t - hi.astype(F32)
        mid = r1.astype(BF16)
        terms += [hi, mid, (r1 - mid.astype(F32)).astype(BF16)]
    tables = lax.dot_general(jnp.concatenate(terms, axis=0), place_ref[...], (((0,), (0,)), ((), ())),
                             preferred_element_type=F32)
    rot_lane = lax.broadcasted_iota(jnp.int32, (1, LANES), 1) & (HEAD_DIM - 1)
    cos = tables[:, :LANES] + jnp.where(rot_lane < ROT_DIM, 0.0, 1.0)
    sa, sb = tables[:, LANES:2 * LANES], tables[:, 2 * LANES:]
    bd = bd_ref[...]
    chunks_per_group = DIL_GW // LANES
    uses = [0]

    def emit(o_refs, j, val):
        gi = j // chunks_per_group
        scr = scratch[uses[0] % len(scratch)]
        uses[0] += 1
        _store_residue_major(o_refs[gi], (j % chunks_per_group) * LANES, val, scr, DIL_GROUPS[gi][1])

    def norm_rope(w_ref, gain_ref, o_refs):
        y = proj(w_ref)
        gain = gain_ref[...]
        for j in range(DIL_W // LANES):
            yc = y[:, j * LANES:(j + 1) * LANES]
            ss = jnp.dot((yc * yc).astype(BF16), bd, preferred_element_type=F32)
            yn = yc * lax.rsqrt(ss * (1.0 / HEAD_DIM) + EPS) * gain
            rot = yn * cos + pltpu.roll(yn, LANES - ROT_DIM // 2, axis=1) * sa + pltpu.roll(yn, ROT_DIM // 2, axis=1) * sb
            emit(o_refs, j, rot)

    norm_rope(wqb_ref, qn_ref, q_refs)
    norm_rope(wkb_ref, kn_ref, k_refs)
    yv = proj(wvb_ref)
    for j in range(DIL_W // LANES):
        emit(v_refs, j, yv[:, j * LANES:(j + 1) * LANES])

    za = proj(wza_ref).astype(BF16)
    gate = jnp.dot(za, wgg_ref[...], preferred_element_type=F32) + bgg_ref[...]
    la_ref[...] = jax.nn.log_sigmoid(gate) * (LOG2_E / GLA_TAU)
    ra_ref[...] = jax.nn.silu(proj(wra_ref)).astype(BF16)
    qa_ref[...] = proj(wqa_ref).astype(BF16)
    ka_ref[...] = proj(wka_ref).astype(BF16)
    va_ref[...] = proj(wva_ref).astype(BF16)


def _in_projection(x2, positions, batch, seq, norm_mix, w_in, w_gla_gate, b_gla_gate, dil_q_norm, dil_k_norm):
    tm = TM_PROJ
    ns = seq // tm
    offs = [0]
    for s in IN_SIZES:
        offs.append(offs[-1] + s)
    wqa, wka, wva, wra, wza, wqb, wkb, wvb = (w_in[:, offs[i]:offs[i + 1]].astype(BF16) for i in range(8))
    wza = jnp.pad(wza, ((0, 0), (0, LANES - GLA_GATE_RANK)))
    wgg = jnp.pad(w_gla_gate.astype(BF16), ((0, LANES - GLA_GATE_RANK), (0, 0)))
    lane = jnp.arange(LANES)
    bd = (lane[:, None] // HEAD_DIM == lane[None, :] // HEAD_DIM).astype(BF16)
    qn = jnp.tile(dil_q_norm * DIL_Q_SCALE, LANES // HEAD_DIM).reshape(1, LANES)
    kn = jnp.tile(dil_k_norm, LANES // HEAD_DIM).reshape(1, LANES)
    half = ROT_DIM // 2
    invf = (ROPE_THETA ** (-jnp.arange(0, ROT_DIM, 2, dtype=F32) / ROT_DIM)).reshape(half, 1)
    rot_lane = jnp.arange(LANES) % HEAD_DIM
    freq = jnp.arange(half)[:, None]
    cos_place = ((rot_lane[None, :] % half == freq) & (rot_lane[None, :] < ROT_DIM)).astype(F32)
    sa_place = -((rot_lane[None, :] == freq)).astype(F32)
    sb_place = ((rot_lane[None, :] == freq + half)).astype(F32)
    zero = jnp.zeros_like(cos_place)
    place = jnp.concatenate([jnp.tile(jnp.concatenate([cos_place, zero, zero], axis=1), (3, 1)),
                             jnp.tile(jnp.concatenate([zero, sa_place, sb_place], axis=1), (3, 1))], axis=0).astype(BF16)
    weights = [invf, place, norm_mix.reshape(1, D_MODEL), wqa, wka, wva, wra, wza, wqb, wkb, wvb, wgg,
               b_gla_gate.reshape(1, GLA_QK), qn, kn, bd]

    def row(w):
        return pl.BlockSpec((tm, w), lambda b, s: (b * ns + s, 0))

    t = batch * seq
    flat_w = [(GLA_QK, BF16), (GLA_QK, BF16), (GLA_V, BF16), (GLA_V, BF16), (GLA_QK, F32)]
    out_specs = [row(w) for w, _ in flat_w]
    out_shape = [jax.ShapeDtypeStruct((t, w), dt) for w, dt in flat_w]
    for _ in range(3):
        for _, dil in DIL_GROUPS:
            out_specs.append(pl.BlockSpec((None, dil, tm // dil, DIL_GW), lambda b, s: (b, 0, s, 0)))
            out_shape.append(jax.ShapeDtypeStruct((batch, dil, seq // dil, DIL_GW), BF16))
    outs = pl.pallas_call(
        _inproj_kernel, grid=(batch, ns),
        in_specs=[row(D_MODEL), pl.BlockSpec((None, 1, tm), lambda b, s: (b * ns + s, 0, 0))]
        + [_resident(w.shape) for w in weights],
        out_specs=out_specs, out_shape=out_shape,
        scratch_shapes=[pltpu.VMEM((tm, LANES), F32)] * 4,
        compiler_params=_params(("parallel", "parallel")), name="in_projection",
    )(x2, positions.reshape(batch * ns, 1, tm), *weights)
    n = len(DIL_GROUPS)
    return outs[:5], outs[5:5 + n], outs[5 + n:5 + 2 * n], outs[5 + 2 * n:]


def _gla_level_refs(g):
    c = GLA_CHUNK
    refs = []
    for h in (32, 16, 8):
        parts = [jnp.broadcast_to(g[m:m + 1, :], (2 * h, g.shape[1])) for m in range(h, c, 2 * h)]
        refs.append(parts[0] if len(parts) == 1 else jnp.concatenate(parts, axis=0))
    g3 = g.reshape(c // 8, 8, g.shape[1])
    sub = lax.broadcasted_iota(jnp.int32, g3.shape, 1)

    def pick(s):
        return jnp.broadcast_to(g3[:, s:s + 1, :], g3.shape)

    refs.append(pick(4).reshape(g.shape))
    refs.append(jnp.where(sub < 4, pick(2), pick(6)).reshape(g.shape))
    r1 = jnp.where(sub < 2, pick(1), jnp.where(sub < 4, pick(3), jnp.where(sub < 6, pick(5), pick(7))))
    refs.append(r1.reshape(g.shape))
    return refs


def _gla_pair_masks():
    c = GLA_CHUNK
    ri = jnp.arange(c)[:, None]
    ci = jnp.arange(c)[None, :]
    slabs = [((ri >> (b + 1)) == (ci >> (b + 1))) & (((ri >> b) & 1) == 1) & (((ci >> b) & 1) == 0)
             for b in (5, 4, 3, 2, 1, 0)]
    slabs.append(ri == ci)
    return jnp.stack(slabs).astype(F32)


def _gla_kernel(qa_ref, ka_ref, va_ref, ra_ref, la_ref, gn_ref, pm_ref, o_ref, state_ref):
    c = GLA_CHUNK
    nb = qa_ref.shape[0]

    @pl.when(pl.program_id(1) == 0)
    def _():
        state_ref[...] = jnp.zeros_like(state_ref)

    ri = lax.broadcasted_iota(jnp.int32, (c, c), 0)
    ci = lax.broadcasted_iota(jnp.int32, (c, c), 1)
    tril = (ri >= ci).astype(BF16)
    row = lax.broadcasted_iota(jnp.int32, (c, GLA_DK), 0)
    upper = [((row >> b) & 1) == 1 for b in (5, 4, 3, 2, 1, 0)]
    gn = gn_ref[...]

    def chunk(ic, carry):
        rows = pl.ds(pl.multiple_of(ic * c, c), c)
        items = [(bb, h) for bb in range(nb) for h in range(GLA_HEADS)]
        kcol = lambda h: slice(h * GLA_DK, (h + 1) * GLA_DK)
        vcol = lambda h: slice(h * GLA_DV, (h + 1) * GLA_DV)
        nt = (((1,), (1,)), ((), ()))
        g_all = []
        for bb in range(nb):
            la = la_ref[bb, rows, :]
            la_hi = la.astype(BF16)
            la_lo = (la - la_hi.astype(F32)).astype(BF16)
            g_all.append(jnp.dot(tril, la_hi, preferred_element_type=F32)
                         + jnp.dot(tril, la_lo, preferred_element_type=F32))
        qs = [qa_ref[bb, rows, kcol(h)].astype(F32) * (GLA_DK ** -0.5) for bb, h in items]
        ks_ = [ka_ref[bb, rows, kcol(h)].astype(F32) for bb, h in items]
        vs_ = [va_ref[bb, rows, vcol(h)] for bb, h in items]
        gs = [g_all[bb][:, kcol(h)] for bb, h in items]
        sts = [state_ref[bb, h] for bb, h in items]
        n = range(len(items))
        o_inter = [lax.dot_general((qs[i] * jnp.exp2(gs[i])).astype(BF16), sts[i].astype(BF16), nt,
                                   preferred_element_type=F32) for i in n]
        grams, diags = [], []
        for i in n:
            zs = []
            for up, gm in zip(upper, _gla_level_refs(gs[i])):
                zs.append((jnp.where(up, qs[i], ks_[i]) * jnp.exp2(-jnp.abs(gs[i] - gm))).astype(BF16))
            z = jnp.stack(zs)
            grams.append(jnp.einsum('lik,ljk->lij', z, z, preferred_element_type=F32))
            diags.append(lax.dot_general(qs[i].astype(BF16), ks_[i].astype(BF16), nt, preferred_element_type=F32))
        n_levels = len(upper)
        attn = []
        for i in n:
            a = pm_ref[n_levels] * diags[i]
            for l in range(n_levels):
                a = a + pm_ref[l] * grams[i][l]
            attn.append(a.astype(BF16))
        outs = [o_inter[i] + jnp.dot(attn[i], vs_[i], preferred_element_type=F32) for i in n]
        for i, (bb, h) in enumerate(items):
            g_last = gs[i][c - 1:c, :]
            kd = (ks_[i] * jnp.exp2(g_last - gs[i])).astype(BF16)
            state_ref[bb, h] = sts[i] * jnp.exp2(g_last) + lax.dot_general(
                vs_[i], kd, (((0,), (0,)), ((), ())), preferred_element_type=F32)
        for i, (bb, h) in enumerate(items):
            o = _rms(outs[i], gn)
            o_ref[bb, rows, vcol(h)] = (o * ra_ref[bb, rows, vcol(h)].astype(F32)).astype(BF16)
        return carry

    lax.fori_loop(0, qa_ref.shape[1] // c, chunk, 0)


def _gla(qa, ka, va, ra, la, gla_out_norm, batch, seq):
    ts = TS_GLA
    nb = GLA_SEQS
    assert batch % nb == 0 and seq % ts == 0

    def view(a):
        return a.reshape(batch, seq, a.shape[-1])

    def blk(w):
        return pl.BlockSpec((nb, ts, w), lambda b, s: (b, s, 0))

    out = pl.pallas_call(
        _gla_kernel, grid=(batch // nb, seq // ts),
        in_specs=[blk(GLA_QK), blk(GLA_QK), blk(GLA_V), blk(GLA_V), blk(GLA_QK),
                  _resident((1, GLA_DV)), _resident((7, GLA_CHUNK, GLA_CHUNK))],
        out_specs=blk(GLA_V),
        out_shape=jax.ShapeDtypeStruct((batch, seq, GLA_V), BF16),
        scratch_shapes=[pltpu.VMEM((nb, GLA_HEADS, GLA_DV, GLA_DK), F32)],
        compiler_params=_params(("parallel", "arbitrary")), name="gla",
    )(view(qa), view(ka), view(va), view(ra), view(la), gla_out_norm.reshape(1, GLA_DV), _gla_pair_masks())
    return out.reshape(batch * seq, GLA_V)


def _dil_kernel(q_ref, kp_ref, kc_ref, vp_ref, vc_ref, acc_ref, m_ref, l_ref):
    blk = DIL_BLOCK
    has_prev = pl.program_id(2) > 0
    qi = lax.broadcasted_iota(jnp.int32, (blk, 2 * blk), 0)
    kj = lax.broadcasted_iota(jnp.int32, (blk, 2 * blk), 1)
    dist = qi + blk - kj
    band = (dist >= 0) & (dist <= blk)
    band_first = band & ((kj >= blk) | has_prev)
    lane = lax.broadcasted_iota(jnp.int32, (blk, LANES), 1)
    stat_slot = lane // STAT_LANES
    nt = (((1,), (1,)), ((), ()))
    pairs = DIL_GW // LANES
    for res, sb in [(res, sb) for res in range(q_ref.shape[0]) for sb in range(q_ref.shape[1] // blk)]:
        rows = slice(sb * blk, (sb + 1) * blk)
        before = slice((sb - 1) * blk, sb * blk)
        mask = band_first if sb == 0 else band
        scores, values = [], []
        for p in range(pairs):
            cs = slice(p * LANES, (p + 1) * LANES)
            q2 = q_ref[res, rows, cs]
            kp, vp = ((kp_ref[res, :, cs], vp_ref[res, :, cs]) if sb == 0
                      else (kc_ref[res, before, cs], vc_ref[res, before, cs]))
            kk = jnp.concatenate([kp, kc_ref[res, rows, cs]], axis=0)
            values.append(jnp.concatenate([vp, vc_ref[res, rows, cs]], axis=0))
            for hh in range(LANES // HEAD_DIM):
                mine = (lane < HEAD_DIM) if hh == 0 else (lane >= HEAD_DIM)
                qm = jnp.where(mine, q2, jnp.zeros_like(q2))
                scores.append(lax.dot_general(qm, kk, nt, preferred_element_type=F32))
        probs = []
        m_all = jnp.zeros((blk, LANES), F32)
        l_all = jnp.zeros((blk, LANES), F32)
        for idx, s in enumerate(scores):
            s = jnp.where(mask, s, -jnp.inf)
            m = jnp.max(s, axis=-1, keepdims=True)
            e = jnp.exp2(s - m)
            l = jnp.sum(e, axis=-1, keepdims=True)
            probs.append(e.astype(BF16))
            m_all = jnp.where(stat_slot == idx, m, m_all)
            l_all = jnp.where(stat_slot == idx, l, l_all)
        m_ref[res, rows, :] = m_all
        l_ref[res, rows, :] = l_all
        for p in range(pairs):
            acc0 = jnp.dot(probs[2 * p], values[p], preferred_element_type=F32)
            acc1 = jnp.dot(probs[2 * p + 1], values[p], preferred_element_type=F32)
            acc_ref[res, rows, p * LANES:(p + 1) * LANES] = jnp.where(lane < HEAD_DIM, acc0, acc1).astype(BF16)


def _dilated_group(q, k, v):
    batch, dil, length, _ = q.shape
    per_step = min(DIL_Q_BLOCKS, length // DIL_BLOCK)
    steps = length // (per_step * DIL_BLOCK)
    residues = DIL_Q_BLOCKS // per_step

    def cur(w):
        return pl.BlockSpec((None, residues, per_step * DIL_BLOCK, w), lambda b, r, n: (b, r, n, 0))

    prev = pl.BlockSpec((None, residues, DIL_BLOCK, DIL_GW), lambda b, r, n: (b, r, jnp.maximum(per_step * n - 1, 0), 0))
    stat = jax.ShapeDtypeStruct((batch, dil, length, LANES), F32)
    return pl.pallas_call(
        _dil_kernel, grid=(batch, dil // residues, steps),
        in_specs=[cur(DIL_GW), prev, cur(DIL_GW), prev, cur(DIL_GW)],
        out_specs=[cur(DIL_GW), cur(LANES), cur(LANES)],
        out_shape=[jax.ShapeDtypeStruct(q.shape, BF16), stat, stat],
        compiler_params=_params(("parallel", "parallel", "arbitrary")), name=f"dilated_attn_d{dil}",
    )(q, k, k, v, v)


def _load_token_major(src_ref, col, scr_ref, dil):
    cs = slice(col, col + LANES)
    if dil == 1:
        return src_ref[0, :, cs].astype(F32)
    n = src_ref.shape[1]
    for r in range(dil):
        scr_ref[pl.ds(r, n, stride=dil), :] = src_ref[r, :, cs].astype(F32)
    return scr_ref[...]


def _merge_kernel(x_ref, oa_ref, a0_ref, a1_ref, a2_ref, m0_ref, m1_ref, m2_ref, l0_ref, l1_ref, l2_ref,
                  nmix_ref, wmg_ref, bmg_ref, wbg_ref, wbd_ref, wmo_ref, h_ref, *scratch):
    x = x_ref[...]
    xn = _rms(x, nmix_ref[...]).astype(BF16)
    gates = jax.nn.sigmoid(jnp.dot(xn, wmg_ref[...], preferred_element_type=F32) + bmg_ref[...])
    br_a = jnp.dot(oa_ref[...], wbg_ref[...], preferred_element_type=F32)
    a_refs, m_refs, l_refs = (a0_ref, a1_ref, a2_ref), (m0_ref, m1_ref, m2_ref), (l0_ref, l1_ref, l2_ref)
    dils = [dil for _, dil in DIL_GROUPS]
    uses = [0]

    def token_major(ref, col, dil):
        uses[0] += 1
        return _load_token_major(ref, col, scratch[uses[0] % len(scratch)], dil)

    ms = [token_major(r, 0, d) for r, d in zip(m_refs, dils)]
    ls = [token_major(r, 0, d) for r, d in zip(l_refs, dils)]
    top = jnp.maximum(jnp.maximum(ms[0], ms[1]), ms[2])
    es = [jnp.exp2(m - top) for m in ms]
    inv = 1.0 / (es[0] * ls[0] + es[1] * ls[1] + es[2] * ls[2])
    def widen(w):
        cols = [jnp.broadcast_to(w[:, STAT_LANES * hd:STAT_LANES * hd + 1], (w.shape[0], HEAD_DIM))
                for hd in range(DIL_HEADS)]
        return jnp.concatenate(cols, axis=1)

    wide = [widen(e * inv) for e in es]
    mixed = []
    for j in range(DIL_GW // LANES):
        cs = slice(j * LANES, (j + 1) * LANES)
        mixed.append(sum(w[:, cs] * token_major(r, j * LANES, d) for w, r, d in zip(wide, a_refs, dils)).astype(BF16))
    br_b = jnp.dot(jnp.concatenate(mixed, axis=1), wbd_ref[...], preferred_element_type=F32)
    merged = gates[:, :D_MODEL] * br_a + gates[:, D_MODEL:] * br_b
    h_ref[...] = x + jnp.dot(merged.astype(BF16), wmo_ref[...], preferred_element_type=F32)


def _merge(x2, oa, accs, ms, ls, batch, seq, norm_mix, w_merge_gate, b_merge_gate, w_br_gla, w_br_dil, w_mix_out):
    tm = TM_MERGE
    ns = seq // tm
    weights = [norm_mix.reshape(1, D_MODEL), w_merge_gate.astype(BF16), b_merge_gate.reshape(1, 2 * D_MODEL),
               w_br_gla.astype(BF16), w_br_dil.astype(BF16), w_mix_out.astype(BF16)]

    def row(w):
        return pl.BlockSpec((tm, w), lambda b, s: (b * ns + s, 0))

    def grouped(w):
        return [pl.BlockSpec((None, dil, tm // dil, w), lambda b, s: (b, 0, s, 0)) for _, dil in DIL_GROUPS]

    return pl.pallas_call(
        _merge_kernel, grid=(batch, ns),
        in_specs=[row(D_MODEL), row(GLA_V)] + grouped(DIL_GW) + grouped(LANES) + grouped(LANES)
        + [_resident(w.shape) for w in weights],
        out_specs=row(D_MODEL), out_shape=jax.ShapeDtypeStruct((batch * seq, D_MODEL), F32),
        scratch_shapes=[pltpu.VMEM((tm, LANES), F32)] * 4,
        compiler_params=_params(("parallel", "parallel")), name="merge",
    )(x2, oa, *accs, *ms, *ls, *weights)


def _memkv_kernel(mem_ref, nmem_ref, wkv_ref, kn_ref, k_ref, v_ref):
    mn = _rms(mem_ref[...], nmem_ref[...]).astype(BF16)
    kv = jnp.dot(mn, wkv_ref[...], preferred_element_type=F32)
    kn = kn_ref[...]
    for h in range(X_HEADS):
        cs = slice(h * X_HEAD_DIM, (h + 1) * X_HEAD_DIM)
        k_ref[:, cs] = _rms(kv[:, cs], kn).astype(BF16)
    v_ref[...] = kv[:, D_MODEL:].astype(BF16)


def _mem_kv(mem, norm_mem, w_xkv, x_k_norm):
    batch, n_mem, _ = mem.shape
    blk = pl.BlockSpec((None, n_mem, D_MODEL), lambda b: (b, 0, 0))
    weights = [norm_mem.reshape(1, D_MODEL), w_xkv.astype(BF16), x_k_norm.reshape(1, X_HEAD_DIM)]
    out = jax.ShapeDtypeStruct((batch, n_mem, D_MODEL), BF16)
    return pl.pallas_call(
        _memkv_kernel, grid=(batch,),
        in_specs=[blk] + [_resident(w.shape) for w in weights],
        out_specs=[blk, blk], out_shape=[out, out],
        compiler_params=_params(("parallel",)), name="mem_kv",
    )(mem, *weights)


def _xattn_kernel(h_ref, k_ref, v_ref, nx_ref, wq_ref, qn_ref, wo_ref, o_ref):
    h = h_ref[...]
    xn = _rms(h, nx_ref[...]).astype(BF16)
    q = jnp.dot(xn, wq_ref[...], preferred_element_type=F32)
    qn = qn_ref[...]
    cols = [slice(hd * X_HEAD_DIM, (hd + 1) * X_HEAD_DIM) for hd in range(X_HEADS)]
    nt = (((1,), (1,)), ((), ()))
    scores = [lax.dot_general((_rms(q[:, cs], qn) * (X_HEAD_DIM ** -0.5 * LOG2_E)).astype(BF16), k_ref[:, cs], nt,
                              preferred_element_type=F32) for cs in cols]
    probs, inv = [], []
    for s in scores:
        e = jnp.exp2(s - jnp.max(s, axis=-1, keepdims=True))
        inv.append(1.0 / jnp.sum(e, axis=-1, keepdims=True))
        probs.append(e.astype(BF16))
    outs = [(jnp.dot(p, v_ref[:, cs], preferred_element_type=F32) * r).astype(BF16)
            for p, r, cs in zip(probs, inv, cols)]
    o = jnp.concatenate(outs, axis=-1)
    o_ref[...] = h + jnp.dot(o, wo_ref[...], preferred_element_type=F32)


def _cross_attention(h1, kmem, vmem, norm_x, w_xq, x_q_norm, w_xo, batch, seq):
    tm = TM_XATTN
    ns = seq // tm
    n_mem = kmem.shape[1]
    weights_a = [norm_x.reshape(1, D_MODEL), w_xq.astype(BF16), x_q_norm.reshape(1, X_HEAD_DIM), w_xo.astype(BF16)]
    row = pl.BlockSpec((tm, D_MODEL), lambda i: (i, 0))
    memblk = pl.BlockSpec((None, n_mem, D_MODEL), lambda i: (i // ns, 0, 0))
    return pl.pallas_call(
        _xattn_kernel, grid=(batch * ns,),
        in_specs=[row, memblk, memblk] + [_resident(w.shape) for w in weights_a],
        out_specs=row, out_shape=jax.ShapeDtypeStruct(h1.shape, F32),
        compiler_params=_params(("parallel",)), name="cross_attention",
    )(h1, kmem, vmem, *weights_a)


def _ffn_kernel(h_ref, nf_ref, wa_ref, wu_ref, wc_ref, bc_ref, wd_ref, o_ref, halo_ref, *stage_refs):
    tm = h_ref.shape[0]
    pad = FFN_HALO

    @pl.when(pl.program_id(1) == 0)
    def _():
        halo_ref[...] = jnp.zeros_like(halo_ref)

    h = h_ref[...]
    xn = _rms(h, nf_ref[...]).astype(BF16)
    acc = jnp.zeros((tm, D_MODEL), F32)
    for j in range(D_FF // FF_CHUNK):
        cs = slice(j * FF_CHUNK, (j + 1) * FF_CHUNK)
        stage = stage_refs[j % len(stage_refs)]
        a = jnp.dot(xn, wa_ref[:, cs], preferred_element_type=F32)
        u = jnp.dot(xn, wu_ref[:, cs], preferred_element_type=F32)
        stage[0:pad, :] = halo_ref[:, cs]
        stage[pad:pad + tm, :] = a
        halo_ref[:, cs] = a[tm - pad:tm, :]
        conv = bc_ref[:, cs] + a * wc_ref[2:3, cs]
        conv = conv + stage[pad - 1:pad - 1 + tm, :] * wc_ref[1:2, cs]
        conv = conv + stage[pad - 2:pad - 2 + tm, :] * wc_ref[0:1, cs]
        gelu = 0.5 * conv * (1.0 + lax.erf(conv * (2.0 ** -0.5)))
        acc = acc + jnp.dot((gelu * u).astype(BF16), wd_ref[cs, :], preferred_element_type=F32)
    o_ref[...] = h + acc


def _ffn(h2, norm_ffn, w_ffn_up, w_ffn_conv, b_ffn_conv, w_ffn_down, batch, seq):
    tm = TM_FFN
    ns = seq // tm
    weights = [norm_ffn.reshape(1, D_MODEL), w_ffn_up[:, :D_FF].astype(BF16), w_ffn_up[:, D_FF:].astype(BF16),
               w_ffn_conv, b_ffn_conv.reshape(1, D_FF), w_ffn_down.astype(BF16)]
    row = pl.BlockSpec((tm, D_MODEL), lambda b, s: (b * ns + s, 0))
    return pl.pallas_call(
        _ffn_kernel, grid=(batch, ns),
        in_specs=[row] + [_resident(w.shape) for w in weights],
        out_specs=row, out_shape=jax.ShapeDtypeStruct(h2.shape, F32),
        scratch_shapes=[pltpu.VMEM((FFN_HALO, D_FF), F32)]
        + [pltpu.VMEM((tm + FFN_HALO, FF_CHUNK), F32)] * min(2, D_FF // FF_CHUNK),
        compiler_params=_params(("parallel", "arbitrary")), name="conv_glu_ffn",
    )(h2, *weights)


def _layer(h, positions, mem, batch, seq, norm_mix, w_in, w_gla_gate, b_gla_gate, gla_out_norm, dil_q_norm, dil_k_norm,
           w_br_gla, w_br_dil, w_merge_gate, b_merge_gate, w_mix_out, norm_x, norm_mem, w_xq, w_xkv, x_q_norm,
           x_k_norm, w_xo, norm_ffn, w_ffn_up, w_ffn_conv, b_ffn_conv, w_ffn_down):
    (qa, ka, va, ra, la), qs, ks, vs = _in_projection(h, positions, batch, seq, norm_mix, w_in, w_gla_gate, b_gla_gate,
                                                      dil_q_norm, dil_k_norm)
    oa = _gla(qa, ka, va, ra, la, gla_out_norm, batch, seq)
    accs, ms, ls = zip(*(_dilated_group(q, k, v) for q, k, v in zip(qs, ks, vs)))
    h1 = _merge(h, oa, accs, ms, ls, batch, seq, norm_mix, w_merge_gate, b_merge_gate, w_br_gla, w_br_dil, w_mix_out)
    kmem, vmem = _mem_kv(mem, norm_mem, w_xkv, x_k_norm)
    h2 = _cross_attention(h1, kmem, vmem, norm_x, w_xq, x_q_norm, w_xo, batch, seq)
    return _ffn(h2, norm_ffn, w_ffn_up, w_ffn_conv, b_ffn_conv, w_ffn_down, batch, seq)


def kernel(x, mem, positions, norm_mix, w_in, w_gla_gate, b_gla_gate, gla_out_norm, dil_q_norm, dil_k_norm, w_br_gla, w_br_dil, w_merge_gate, b_merge_gate, w_mix_out, norm_x, norm_mem, w_xq, w_xkv, x_q_norm, x_k_norm, w_xo, norm_ffn, w_ffn_up, w_ffn_conv, b_ffn_conv, w_ffn_down):
    batch, seq, _ = x.shape
    h = x.reshape(batch * seq, D_MODEL)
    stacked = (norm_mix, w_in, w_gla_gate, b_gla_gate, gla_out_norm, dil_q_norm, dil_k_norm, w_br_gla, w_br_dil,
               w_merge_gate, b_merge_gate, w_mix_out, norm_x, norm_mem, w_xq, w_xkv, x_q_norm, x_k_norm, w_xo,
               norm_ffn, w_ffn_up, w_ffn_conv, b_ffn_conv, w_ffn_down)
    for l in range(norm_mix.shape[0]):
        h = _layer(h, positions, mem, batch, seq, *(p[l] for p in stacked))
    return h.reshape(batch, seq, D_MODEL)
```

```python
import functools

import jax
import jax.numpy as jnp
from jax import lax
from jax.experimental import pallas as pl
from jax.experimental.pallas import tpu as pltpu

F32 = jnp.float32
BF16 = jnp.bfloat16

D_MODEL = 1024
EPS = 1e-6
GLA_HEADS = 4
GLA_DK = 128
GLA_DV = 256
GLA_QK = GLA_HEADS * GLA_DK
GLA_V = GLA_HEADS * GLA_DV
GLA_GATE_RANK = 16
GLA_TAU = 16.0
GLA_CHUNK = 64
DIL_GROUPS = ((128, 1), (512, 4), (2048, 16))
DIL_HEADS = 8
HEAD_DIM = 64
DIL_GW = DIL_HEADS * HEAD_DIM
DIL_W = len(DIL_GROUPS) * DIL_GW
DIL_BLOCK = 128
ROT_DIM = HEAD_DIM // 4
ROPE_THETA = 500000.0
LOG2_E = 1.4426950408889634
DIL_Q_SCALE = HEAD_DIM ** -0.5 * LOG2_E
STAT_LANES = 16
X_HEADS = 4
X_HEAD_DIM = D_MODEL // X_HEADS
D_FF = 2816
CONV_W = 3
IN_SIZES = (GLA_QK, GLA_QK, GLA_V, GLA_V, GLA_GATE_RANK, DIL_W, DIL_W, DIL_W)


def _proj_cols():
    cols, o = {}, 0
    for name, n in (("qa", GLA_QK), ("ka", GLA_QK), ("va", GLA_V), ("ra", GLA_V), ("qb", DIL_W), ("kb", DIL_W),
                    ("vb", DIL_W), ("za", 128)):
        cols[name] = (o, o + n)
        o += n
    return cols


PROJ_COLS = _proj_cols()

LANES = 128
VMEM_LIMIT = 56 * 1024 * 1024

TM_PROJ = 512
TS_GLA = 256
GLA_SEQS = 4
DIL_Q_BLOCKS = 4
TM_MERGE = 512
TM_XATTN = 1024
TM_FFN = 512
FF_CHUNK = D_FF
FFN_HALO = 8


def _params(sem):
    return pltpu.CompilerParams(dimension_semantics=sem, vmem_limit_bytes=VMEM_LIMIT)


def _resident(shape):
    nd = len(shape)
    return pl.BlockSpec(shape, lambda *_: (0,) * nd, pipeline_mode=pl.Buffered(1))


def _rms(x, gain):
    return x * lax.rsqrt(jnp.mean(x * x, axis=-1, keepdims=True) + EPS) * gain


def _store_residue_major(o_ref, col, val, scr_ref, dil):
    cs = slice(col, col + LANES)
    if dil == 1:
        o_ref[0, :, cs] = val.astype(BF16)
        return
    n = val.shape[0] // dil
    scr_ref[...] = val
    for r in range(dil):
        o_ref[r, :, cs] = scr_ref[pl.ds(r, n, stride=dil), :].astype(BF16)


def _inproj_kernel(x_ref, pos_ref, invf_ref, place_ref, nmix_ref, w_ref, wgg_ref, bgg_ref, qn_ref, kn_ref, bd_ref,
                   qa_ref, ka_ref, va_ref, ra_ref, la_ref, *rest):
    n_groups = len(DIL_GROUPS)
    q_refs, k_refs, v_refs = rest[:n_groups], rest[n_groups:2 * n_groups], rest[2 * n_groups:3 * n_groups]
    scratch = rest[3 * n_groups:]
    xn = _rms(x_ref[...], nmix_ref[...]).astype(BF16)

    def proj(name):
        lo, hi = PROJ_COLS[name]
        return jnp.dot(xn, w_ref[:, lo:hi], preferred_element_type=F32)

    ang = invf_ref[...] * pos_ref[...].astype(F32)
    terms = []
    for t in (jnp.cos(ang), jnp.sin(ang)):
        hi = t.astype(BF16)
        r1 = t - hi.astype(F32)
        mid = r1.astype(BF16)
        terms += [hi, mid, (r1 - mid.astype(F32)).astype(BF16)]
    tables = lax.dot_general(jnp.concatenate(terms, axis=0), place_ref[...], (((0,), (0,)), ((), ())),
                             preferred_element_type=F32)
    rot_lane = lax.broadcasted_iota(jnp.int32, (1, LANES), 1) & (HEAD_DIM - 1)
    cos = tables[:, :LANES] + jnp.where(rot_lane < ROT_DIM, 0.0, 1.0)
    sa, sb = tables[:, LANES:2 * LANES], tables[:, 2 * LANES:]
    bd = bd_ref[...]
    chunks_per_group = DIL_GW // LANES
    uses = [0]

    def emit(o_refs, j, val):
        gi = j // chunks_per_group
        scr = scratch[uses[0] % len(scratch)]
        uses[0] += 1
        _store_residue_major(o_refs[gi], (j % chunks_per_group) * LANES, val, scr, DIL_GROUPS[gi][1])

    def norm_rope(name, gain_ref, o_refs):
        y = proj(name)
        gain = gain_ref[...]
        for j in range(DIL_W // LANES):
            yc = y[:, j * LANES:(j + 1) * LANES]
            ss = jnp.dot((yc * yc).astype(BF16), bd, preferred_element_type=F32)
            yn = yc * lax.rsqrt(ss * (1.0 / HEAD_DIM) + EPS) * gain
            rot = yn * cos + pltpu.roll(yn, LANES - ROT_DIM // 2, axis=1) * sa + pltpu.roll(yn, ROT_DIM // 2, axis=1) * sb
            emit(o_refs, j, rot)

    norm_rope("qb", qn_ref, q_refs)
    norm_rope("kb", kn_ref, k_refs)
    yv = proj("vb")
    for j in range(DIL_W // LANES):
        emit(v_refs, j, yv[:, j * LANES:(j + 1) * LANES])

    za = proj("za").astype(BF16)
    gate = jnp.dot(za, wgg_ref[...], preferred_element_type=F32) + bgg_ref[...]
    la_ref[...] = jax.nn.log_sigmoid(gate) * (LOG2_E / GLA_TAU)
    ra_ref[...] = jax.nn.silu(proj("ra")).astype(BF16)
    qa_ref[...] = proj("qa").astype(BF16)
    ka_ref[...] = proj("ka").astype(BF16)
    va_ref[...] = proj("va").astype(BF16)


def _in_projection(x2, positions, batch, seq, norm_mix, w_in, w_gla_gate, b_gla_gate, dil_q_norm, dil_k_norm):
    tm = TM_PROJ
    ns = seq // tm
    gla_cols = sum(IN_SIZES[:4])
    w_all = jnp.concatenate([w_in[:, :gla_cols], w_in[:, gla_cols + GLA_GATE_RANK:],
                             w_in[:, gla_cols:gla_cols + GLA_GATE_RANK],
                             jnp.zeros((D_MODEL, LANES - GLA_GATE_RANK), w_in.dtype)], axis=1).astype(BF16)
    wgg = jnp.pad(w_gla_gate.astype(BF16), ((0, LANES - GLA_GATE_RANK), (0, 0)))
    lane = jnp.arange(LANES)
    bd = (lane[:, None] // HEAD_DIM == lane[None, :] // HEAD_DIM).astype(BF16)
    qn = jnp.tile(dil_q_norm * DIL_Q_SCALE, LANES // HEAD_DIM).reshape(1, LANES)
    kn = jnp.tile(dil_k_norm, LANES // HEAD_DIM).reshape(1, LANES)
    half = ROT_DIM // 2
    invf = (ROPE_THETA ** (-jnp.arange(0, ROT_DIM, 2, dtype=F32) / ROT_DIM)).reshape(half, 1)
    rot_lane = jnp.arange(LANES) % HEAD_DIM
    freq = jnp.arange(half)[:, None]
    cos_place = ((rot_lane[None, :] % half == freq) & (rot_lane[None, :] < ROT_DIM)).astype(F32)
    sa_place = -((rot_lane[None, :] == freq)).astype(F32)
    sb_place = ((rot_lane[None, :] == freq + half)).astype(F32)
    zero = jnp.zeros_like(cos_place)
    place = jnp.concatenate([jnp.tile(jnp.concatenate([cos_place, zero, zero], axis=1), (3, 1)),
                             jnp.tile(jnp.concatenate([zero, sa_place, sb_place], axis=1), (3, 1))], axis=0).astype(BF16)
    weights = [invf, place, norm_mix.reshape(1, D_MODEL), w_all, wgg,
               b_gla_gate.reshape(1, GLA_QK), qn, kn, bd]

    def row(w):
        return pl.BlockSpec((tm, w), lambda b, s: (b * ns + s, 0))

    t = batch * seq
    flat_w = [(GLA_QK, BF16), (GLA_QK, BF16), (GLA_V, BF16), (GLA_V, BF16), (GLA_QK, F32)]
    out_specs = [row(w) for w, _ in flat_w]
    out_shape = [jax.ShapeDtypeStruct((t, w), dt) for w, dt in flat_w]
    for _ in range(3):
        for _, dil in DIL_GROUPS:
            out_specs.append(pl.BlockSpec((None, dil, tm // dil, DIL_GW), lambda b, s: (b, 0, s, 0)))
            out_shape.append(jax.ShapeDtypeStruct((batch, dil, seq // dil, DIL_GW), BF16))
    outs = pl.pallas_call(
        _inproj_kernel, grid=(batch, ns),
        in_specs=[row(D_MODEL), pl.BlockSpec((None, 1, tm), lambda b, s: (b * ns + s, 0, 0))]
        + [_resident(w.shape) for w in weights],
        out_specs=out_specs, out_shape=out_shape,
        scratch_shapes=[pltpu.VMEM((tm, LANES), F32)] * 4,
        compiler_params=_params(("parallel", "parallel")), name="in_projection",
    )(x2, positions.reshape(batch * ns, 1, tm), *weights)
    n = len(DIL_GROUPS)
    return outs[:5], outs[5:5 + n], outs[5 + n:5 + 2 * n], outs[5 + 2 * n:]


def _gla_level_refs(g):
    c = GLA_CHUNK
    refs = []
    for h in (32, 16, 8):
        parts = [jnp.broadcast_to(g[m:m + 1, :], (2 * h, g.shape[1])) for m in range(h, c, 2 * h)]
        refs.append(parts[0] if len(parts) == 1 else jnp.concatenate(parts, axis=0))
    g3 = g.reshape(c // 8, 8, g.shape[1])
    sub = lax.broadcasted_iota(jnp.int32, g3.shape, 1)

    def pick(s):
        return jnp.broadcast_to(g3[:, s:s + 1, :], g3.shape)

    refs.append(pick(4).reshape(g.shape))
    refs.append(jnp.where(sub < 4, pick(2), pick(6)).reshape(g.shape))
    r1 = jnp.where(sub < 2, pick(1), jnp.where(sub < 4, pick(3), jnp.where(sub < 6, pick(5), pick(7))))
    refs.append(r1.reshape(g.shape))
    return refs


def _gla_pair_masks():
    c = GLA_CHUNK
    ri = jnp.arange(c)[:, None]
    ci = jnp.arange(c)[None, :]
    slabs = [((ri >> (b + 1)) == (ci >> (b + 1))) & (((ri >> b) & 1) == 1) & (((ci >> b) & 1) == 0)
             for b in (5, 4, 3, 2, 1, 0)]
    slabs.append(ri == ci)
    return jnp.stack(slabs).astype(F32)


def _gla_kernel(qa_ref, ka_ref, va_ref, ra_ref, la_ref, gn_ref, pm_ref, o_ref, state_ref):
    c = GLA_CHUNK
    nb = qa_ref.shape[0]

    @pl.when(pl.program_id(1) == 0)
    def _():
        state_ref[...] = jnp.zeros_like(state_ref)

    ri = lax.broadcasted_iota(jnp.int32, (c, c), 0)
    ci = lax.broadcasted_iota(jnp.int32, (c, c), 1)
    tril = (ri >= ci).astype(BF16)
    row = lax.broadcasted_iota(jnp.int32, (c, GLA_DK), 0)
    upper = [((row >> b) & 1) == 1 for b in (5, 4, 3, 2, 1, 0)]
    gn = gn_ref[...]

    def chunk(ic, carry):
        rows = pl.ds(pl.multiple_of(ic * c, c), c)
        items = [(bb, h) for bb in range(nb) for h in range(GLA_HEADS)]
        kcol = lambda h: slice(h * GLA_DK, (h + 1) * GLA_DK)
        vcol = lambda h: slice(h * GLA_DV, (h + 1) * GLA_DV)
        nt = (((1,), (1,)), ((), ()))
        g_all = []
        for bb in range(nb):
            la = la_ref[bb, rows, :]
            la_hi = la.astype(BF16)
            la_lo = (la - la_hi.astype(F32)).astype(BF16)
            g_all.append(jnp.dot(tril, la_hi, preferred_element_type=F32)
                         + jnp.dot(tril, la_lo, preferred_element_type=F32))
        qs = [qa_ref[bb, rows, kcol(h)].astype(F32) * (GLA_DK ** -0.5) for bb, h in items]
        ks_ = [ka_ref[bb, rows, kcol(h)].astype(F32) for bb, h in items]
        vs_ = [va_ref[bb, rows, vcol(h)] for bb, h in items]
        gs = [g_all[bb][:, kcol(h)] for bb, h in items]
        sts = [state_ref[bb, h] for bb, h in items]
        n = range(len(items))
        o_inter = [lax.dot_general((qs[i] * jnp.exp2(gs[i])).astype(BF16), sts[i].astype(BF16), nt,
                                   preferred_element_type=F32) for i in n]
        grams, diags = [], []
        for i in n:
            zs = []
            for up, gm in zip(upper, _gla_level_refs(gs[i])):
                zs.append((jnp.where(up, qs[i], ks_[i]) * jnp.exp2(-jnp.abs(gs[i] - gm))).astype(BF16))
            z = jnp.stack(zs)
            grams.append(jnp.einsum('lik,ljk->lij', z, z, preferred_element_type=F32))
            diags.append(lax.dot_general(qs[i].astype(BF16), ks_[i].astype(BF16), nt, preferred_element_type=F32))
        n_levels = len(upper)
        attn = []
        for i in n:
            a = pm_ref[n_levels] * diags[i]
            for l in range(n_levels):
                a = a + pm_ref[l] * grams[i][l]
            attn.append(a.astype(BF16))
        outs = [o_inter[i] + jnp.dot(attn[i], vs_[i], preferred_element_type=F32) for i in n]
        for i, (bb, h) in enumerate(items):
            g_last = gs[i][c - 1:c, :]
            kd = (ks_[i] * jnp.exp2(g_last - gs[i])).astype(BF16)
            state_ref[bb, h] = sts[i] * jnp.exp2(g_last) + lax.dot_general(
                vs_[i], kd, (((0,), (0,)), ((), ())), preferred_element_type=F32)
        for i, (bb, h) in enumerate(items):
            o = _rms(outs[i], gn)
            o_ref[bb, rows, vcol(h)] = (o * ra_ref[bb, rows, vcol(h)].astype(F32)).astype(BF16)
        return carry

    lax.fori_loop(0, qa_ref.shape[1] // c, chunk, 0)


def _gla(qa, ka, va, ra, la, gla_out_norm, batch, seq):
    ts = TS_GLA
    nb = GLA_SEQS
    assert batch % nb == 0 and seq % ts == 0

    def view(a):
        return a.reshape(batch, seq, a.shape[-1])

    def blk(w):
        return pl.BlockSpec((nb, ts, w), lambda b, s: (b, s, 0))

    out = pl.pallas_call(
        _gla_kernel, grid=(batch // nb, seq // ts),
        in_specs=[blk(GLA_QK), blk(GLA_QK), blk(GLA_V), blk(GLA_V), blk(GLA_QK),
                  _resident((1, GLA_DV)), _resident((7, GLA_CHUNK, GLA_CHUNK))],
        out_specs=blk(GLA_V),
        out_shape=jax.ShapeDtypeStruct((batch, seq, GLA_V), BF16),
        scratch_shapes=[pltpu.VMEM((nb, GLA_HEADS, GLA_DV, GLA_DK), F32)],
        compiler_params=_params(("parallel", "arbitrary")), name="gla",
    )(view(qa), view(ka), view(va), view(ra), view(la), gla_out_norm.reshape(1, GLA_DV), _gla_pair_masks())
    return out.reshape(batch * seq, GLA_V)


def _dil_kernel(q_ref, kp_ref, kc_ref, vp_ref, vc_ref, acc_ref, m_ref, l_ref):
    blk = DIL_BLOCK
    has_prev = pl.program_id(2) > 0
    qi = lax.broadcasted_iota(jnp.int32, (blk, 2 * blk), 0)
    kj = lax.broadcasted_iota(jnp.int32, (blk, 2 * blk), 1)
    dist = qi + blk - kj
    band = (dist >= 0) & (dist <= blk)
    band_first = band & ((kj >= blk) | has_prev)
    lane = lax.broadcasted_iota(jnp.int32, (blk, LANES), 1)
    nt = (((1,), (1,)), ((), ()))
    pairs = DIL_GW // LANES
    for res, sb in [(res, sb) for res in range(q_ref.shape[0]) for sb in range(q_ref.shape[1] // blk)]:
        rows = slice(sb * blk, (sb + 1) * blk)
        before = slice((sb - 1) * blk, sb * blk)
        mask = band_first if sb == 0 else band
        scores, values = [], []
        for p in range(pairs):
            cs = slice(p * LANES, (p + 1) * LANES)
            q2 = q_ref[res, rows, cs]
            kp, vp = ((kp_ref[res, :, cs], vp_ref[res, :, cs]) if sb == 0
                      else (kc_ref[res, before, cs], vc_ref[res, before, cs]))
            kk = jnp.concatenate([kp, kc_ref[res, rows, cs]], axis=0)
            values.append(jnp.concatenate([vp, vc_ref[res, rows, cs]], axis=0))
            for hh in range(LANES // HEAD_DIM):
                mine = (lane < HEAD_DIM) if hh == 0 else (lane >= HEAD_DIM)
                qm = jnp.where(mine, q2, jnp.zeros_like(q2))
                scores.append(lax.dot_general(qm, kk, nt, preferred_element_type=F32))
        probs = []
        for idx, s in enumerate(scores):
            s = jnp.where(mask, s, -jnp.inf)
            m = jnp.max(s, axis=-1, keepdims=True)
            e = jnp.exp2(s - m)
            l = jnp.sum(e, axis=-1, keepdims=True)
            probs.append(e.astype(BF16))
            slot = slice(idx * STAT_LANES, (idx + 1) * STAT_LANES)
            m_ref[res, rows, slot] = jnp.broadcast_to(m, (blk, STAT_LANES))
            l_ref[res, rows, slot] = jnp.broadcast_to(l, (blk, STAT_LANES))
        for p in range(pairs):
            acc0 = jnp.dot(probs[2 * p], values[p], preferred_element_type=F32)
            acc1 = jnp.dot(probs[2 * p + 1], values[p], preferred_element_type=F32)
            acc_ref[res, rows, p * LANES:(p + 1) * LANES] = jnp.where(lane < HEAD_DIM, acc0, acc1).astype(BF16)


def _dilated_group(q, k, v):
    batch, dil, length, _ = q.shape
    per_step = min(DIL_Q_BLOCKS, length // DIL_BLOCK)
    steps = length // (per_step * DIL_BLOCK)
    residues = DIL_Q_BLOCKS // per_step

    def cur(w):
        return pl.BlockSpec((None, residues, per_step * DIL_BLOCK, w), lambda b, r, n: (b, r, n, 0))

    prev = pl.BlockSpec((None, residues, DIL_BLOCK, DIL_GW), lambda b, r, n: (b, r, jnp.maximum(per_step * n - 1, 0), 0))
    stat = jax.ShapeDtypeStruct((batch, dil, length, LANES), F32)
    return pl.pallas_call(
        _dil_kernel, grid=(batch, dil // residues, steps),
        in_specs=[cur(DIL_GW), prev, cur(DIL_GW), prev, cur(DIL_GW)],
        out_specs=[cur(DIL_GW), cur(LANES), cur(LANES)],
        out_shape=[jax.ShapeDtypeStruct(q.shape, BF16), stat, stat],
        compiler_params=_params(("parallel", "parallel", "arbitrary")), name=f"dilated_attn_d{dil}",
    )(q, k, k, v, v)


def _load_token_major(src_ref, col, scr_ref, dil):
    cs = slice(col, col + LANES)
    if dil == 1:
        return src_ref[0, :, cs].astype(F32)
    n = src_ref.shape[1]
    for r in range(dil):
        scr_ref[pl.ds(r, n, stride=dil), :] = src_ref[r, :, cs].astype(F32)
    return scr_ref[...]


def _merge_kernel(x_ref, oa_ref, a0_ref, a1_ref, a2_ref, m0_ref, m1_ref, m2_ref, l0_ref, l1_ref, l2_ref,
                  nmix_ref, wmg_ref, bmg_ref, wbg_ref, wbd_ref, wmo_ref, h_ref, *scratch):
    x = x_ref[...]
    xn = _rms(x, nmix_ref[...]).astype(BF16)
    gates = jax.nn.sigmoid(jnp.dot(xn, wmg_ref[...], preferred_element_type=F32) + bmg_ref[...])
    br_a = jnp.dot(oa_ref[...], wbg_ref[...], preferred_element_type=F32)
    a_refs, m_refs, l_refs = (a0_ref, a1_ref, a2_ref), (m0_ref, m1_ref, m2_ref), (l0_ref, l1_ref, l2_ref)
    dils = [dil for _, dil in DIL_GROUPS]
    uses = [0]

    def token_major(ref, col, dil):
        uses[0] += 1
        return _load_token_major(ref, col, scratch[uses[0] % len(scratch)], dil)

    ms = [token_major(r, 0, d) for r, d in zip(m_refs, dils)]
    ls = [token_major(r, 0, d) for r, d in zip(l_refs, dils)]
    top = jnp.maximum(jnp.maximum(ms[0], ms[1]), ms[2])
    es = [jnp.exp2(m - top) for m in ms]
    inv = 1.0 / (es[0] * ls[0] + es[1] * ls[1] + es[2] * ls[2])
    def widen(w):
        cols = [jnp.broadcast_to(w[:, STAT_LANES * hd:STAT_LANES * hd + 1], (w.shape[0], HEAD_DIM))
                for hd in range(DIL_HEADS)]
        return jnp.concatenate(cols, axis=1)

    wide = [widen(e * inv) for e in es]
    mixed = []
    for j in range(DIL_GW // LANES):
        cs = slice(j * LANES, (j + 1) * LANES)
        mixed.append(sum(w[:, cs] * token_major(r, j * LANES, d) for w, r, d in zip(wide, a_refs, dils)).astype(BF16))
    br_b = jnp.dot(jnp.concatenate(mixed, axis=1), wbd_ref[...], preferred_element_type=F32)
    merged = gates[:, :D_MODEL] * br_a + gates[:, D_MODEL:] * br_b
    h_ref[...] = x + jnp.dot(merged.astype(BF16), wmo_ref[...], preferred_element_type=F32)


def _merge(x2, oa, accs, ms, ls, batch, seq, norm_mix, w_merge_gate, b_merge_gate, w_br_gla, w_br_dil, w_mix_out):
    tm = TM_MERGE
    ns = seq // tm
    weights = [norm_mix.reshape(1, D_MODEL), w_merge_gate.astype(BF16), b_merge_gate.reshape(1, 2 * D_MODEL),
               w_br_gla.astype(BF16), w_br_dil.astype(BF16), w_mix_out.astype(BF16)]

    def row(w):
        return pl.BlockSpec((tm, w), lambda b, s: (b * ns + s, 0))

    def grouped(w):
        return [pl.BlockSpec((None, dil, tm // dil, w), lambda b, s: (b, 0, s, 0)) for _, dil in DIL_GROUPS]

    return pl.pallas_call(
        _merge_kernel, grid=(batch, ns),
        in_specs=[row(D_MODEL), row(GLA_V)] + grouped(DIL_GW) + grouped(LANES) + grouped(LANES)
        + [_resident(w.shape) for w in weights],
        out_specs=row(D_MODEL), out_shape=jax.ShapeDtypeStruct((batch * seq, D_MODEL), F32),
        scratch_shapes=[pltpu.VMEM((tm, LANES), F32)] * 4,
        compiler_params=_params(("parallel", "parallel")), name="merge",
    )(x2, oa, *accs, *ms, *ls, *weights)


def _memkv_kernel(mem_ref, nmem_ref, wkv_ref, kn_ref, k_ref, v_ref):
    mn = _rms(mem_ref[...], nmem_ref[...]).astype(BF16)
    kv = jnp.dot(mn, wkv_ref[...], preferred_element_type=F32)
    kn = kn_ref[...]
    for h in range(X_HEADS):
        cs = slice(h * X_HEAD_DIM, (h + 1) * X_HEAD_DIM)
        k_ref[:, cs] = _rms(kv[:, cs], kn).astype(BF16)
    v_ref[...] = kv[:, D_MODEL:].astype(BF16)


def _mem_kv(mem, norm_mem, w_xkv, x_k_norm):
    batch, n_mem, _ = mem.shape
    blk = pl.BlockSpec((None, n_mem, D_MODEL), lambda b: (b, 0, 0))
    weights = [norm_mem.reshape(1, D_MODEL), w_xkv.astype(BF16), x_k_norm.reshape(1, X_HEAD_DIM)]
    out = jax.ShapeDtypeStruct((batch, n_mem, D_MODEL), BF16)
    return pl.pallas_call(
        _memkv_kernel, grid=(batch,),
        in_specs=[blk] + [_resident(w.shape) for w in weights],
        out_specs=[blk, blk], out_shape=[out, out],
        compiler_params=_params(("parallel",)), name="mem_kv",
    )(mem, *weights)


def _xattn_kernel(h_ref, k_ref, v_ref, nx_ref, wq_ref, qn_ref, wo_ref, o_ref):
    h = h_ref[...]
    xn = _rms(h, nx_ref[...]).astype(BF16)
    q = jnp.dot(xn, wq_ref[...], preferred_element_type=F32)
    qn = qn_ref[...]
    cols = [slice(hd * X_HEAD_DIM, (hd + 1) * X_HEAD_DIM) for hd in range(X_HEADS)]
    nt = (((1,), (1,)), ((), ()))
    scores = [lax.dot_general((_rms(q[:, cs], qn) * (X_HEAD_DIM ** -0.5 * LOG2_E)).astype(BF16), k_ref[:, cs], nt,
                              preferred_element_type=F32) for cs in cols]
    probs, inv = [], []
    for s in scores:
        e = jnp.exp2(s - jnp.max(s, axis=-1, keepdims=True))
        inv.append(1.0 / jnp.sum(e, axis=-1, keepdims=True))
        probs.append(e.astype(BF16))
    outs = [(jnp.dot(p, v_ref[:, cs], preferred_element_type=F32) * r).astype(BF16)
            for p, r, cs in zip(probs, inv, cols)]
    o = jnp.concatenate(outs, axis=-1)
    o_ref[...] = h + jnp.dot(o, wo_ref[...], preferred_element_type=F32)


def _cross_attention(h1, kmem, vmem, norm_x, w_xq, x_q_norm, w_xo, batch, seq):
    tm = TM_XATTN
    ns = seq // tm
    n_mem = kmem.shape[1]
    weights_a = [norm_x.reshape(1, D_MODEL), w_xq.astype(BF16), x_q_norm.reshape(1, X_HEAD_DIM), w_xo.astype(BF16)]
    row = pl.BlockSpec((tm, D_MODEL), lambda i: (i, 0))
    memblk = pl.BlockSpec((None, n_mem, D_MODEL), lambda i: (i // ns, 0, 0))
    return pl.pallas_call(
        _xattn_kernel, grid=(batch * ns,),
        in_specs=[row, memblk, memblk] + [_resident(w.shape) for w in weights_a],
        out_specs=row, out_shape=jax.ShapeDtypeStruct(h1.shape, F32),
        compiler_params=_params(("parallel",)), name="cross_attention",
    )(h1, kmem, vmem, *weights_a)


def _ffn_kernel(h_ref, nf_ref, wup_ref, wc_ref, bc_ref, wd_ref, o_ref, halo_ref, *stage_refs):
    tm = h_ref.shape[0]
    pad = FFN_HALO

    @pl.when(pl.program_id(1) == 0)
    def _():
        halo_ref[...] = jnp.zeros_like(halo_ref)

    h = h_ref[...]
    xn = _rms(h, nf_ref[...]).astype(BF16)
    acc = jnp.zeros((tm, D_MODEL), F32)
    for j in range(D_FF // FF_CHUNK):
        cs = slice(j * FF_CHUNK, (j + 1) * FF_CHUNK)
        stage = stage_refs[j % len(stage_refs)]
        a = jnp.dot(xn, wup_ref[:, cs], preferred_element_type=F32)
        u = jnp.dot(xn, wup_ref[:, D_FF + j * FF_CHUNK:D_FF + (j + 1) * FF_CHUNK], preferred_element_type=F32)
        stage[0:pad, :] = halo_ref[:, cs]
        stage[pad:pad + tm, :] = a
        halo_ref[:, cs] = a[tm - pad:tm, :]
        conv = bc_ref[:, cs] + a * wc_ref[2:3, cs]
        conv = conv + stage[pad - 1:pad - 1 + tm, :] * wc_ref[1:2, cs]
        conv = conv + stage[pad - 2:pad - 2 + tm, :] * wc_ref[0:1, cs]
        gelu = 0.5 * conv * (1.0 + lax.erf(conv * (2.0 ** -0.5)))
        acc = acc + jnp.dot((gelu * u).astype(BF16), wd_ref[cs, :], preferred_element_type=F32)
    o_ref[...] = h + acc


def _ffn(h2, norm_ffn, w_ffn_up, w_ffn_conv, b_ffn_conv, w_ffn_down, batch, seq):
    tm = TM_FFN
    ns = seq // tm
    weights = [norm_ffn.reshape(1, D_MODEL), w_ffn_up.astype(BF16),
               w_ffn_conv, b_ffn_conv.reshape(1, D_FF), w_ffn_down.astype(BF16)]
    row = pl.BlockSpec((tm, D_MODEL), lambda b, s: (b * ns + s, 0))
    return pl.pallas_call(
        _ffn_kernel, grid=(batch, ns),
        in_specs=[row] + [_resident(w.shape) for w in weights],
        out_specs=row, out_shape=jax.ShapeDtypeStruct(h2.shape, F32),
        scratch_shapes=[pltpu.VMEM((FFN_HALO, D_FF), F32)]
        + [pltpu.VMEM((tm + FFN_HALO, FF_CHUNK), F32)] * min(2, D_FF // FF_CHUNK),
        compiler_params=_params(("parallel", "arbitrary")), name="conv_glu_ffn",
    )(h2, *weights)


def _layer(h, positions, mem, batch, seq, norm_mix, w_in, w_gla_gate, b_gla_gate, gla_out_norm, dil_q_norm, dil_k_norm,
           w_br_gla, w_br_dil, w_merge_gate, b_merge_gate, w_mix_out, norm_x, norm_mem, w_xq, w_xkv, x_q_norm,
           x_k_norm, w_xo, norm_ffn, w_ffn_up, w_ffn_conv, b_ffn_conv, w_ffn_down):
    (qa, ka, va, ra, la), qs, ks, vs = _in_projection(h, positions, batch, seq, norm_mix, w_in, w_gla_gate, b_gla_gate,
                                                      dil_q_norm, dil_k_norm)
    oa = _gla(qa, ka, va, ra, la, gla_out_norm, batch, seq)
    accs, ms, ls = zip(*(_dilated_group(q, k, v) for q, k, v in zip(qs, ks, vs)))
    h1 = _merge(h, oa, accs, ms, ls, batch, seq, norm_mix, w_merge_gate, b_merge_gate, w_br_gla, w_br_dil, w_mix_out)
    kmem, vmem = _mem_kv(mem, norm_mem, w_xkv, x_k_norm)
    h2 = _cross_attention(h1, kmem, vmem, norm_x, w_xq, x_q_norm, w_xo, batch, seq)
    return _ffn(h2, norm_ffn, w_ffn_up, w_ffn_conv, b_ffn_conv, w_ffn_down, batch, seq)


def kernel(x, mem, positions, norm_mix, w_in, w_gla_gate, b_gla_gate, gla_out_norm, dil_q_norm, dil_k_norm, w_br_gla, w_br_dil, w_merge_gate, b_merge_gate, w_mix_out, norm_x, norm_mem, w_xq, w_xkv, x_q_norm, x_k_norm, w_xo, norm_ffn, w_ffn_up, w_ffn_conv, b_ffn_conv, w_ffn_down):
    batch, seq, _ = x.shape
    h = x.reshape(batch * seq, D_MODEL)
    stacked = (norm_mix, w_in, w_gla_gate, b_gla_gate, gla_out_norm, dil_q_norm, dil_k_norm, w_br_gla, w_br_dil,
               w_merge_gate, b_merge_gate, w_mix_out, norm_x, norm_mem, w_xq, w_xkv, x_q_norm, x_k_norm, w_xo,
               norm_ffn, w_ffn_up, w_ffn_conv, b_ffn_conv, w_ffn_down)
    for l in range(norm_mix.shape[0]):
        h = _layer(h, positions, mem, batch, seq, *(p[l] for p in stacked))
    return h.reshape(batch, seq, D_MODEL)
```

```python
import functools

import jax
import jax.numpy as jnp
from jax import lax
from jax.experimental import pallas as pl
from jax.experimental.pallas import tpu as pltpu

F32 = jnp.float32
BF16 = jnp.bfloat16

D_MODEL = 1024
EPS = 1e-6
GLA_HEADS = 4
GLA_DK = 128
GLA_DV = 256
GLA_QK = GLA_HEADS * GLA_DK
GLA_V = GLA_HEADS * GLA_DV
GLA_GATE_RANK = 16
GLA_TAU = 16.0
GLA_CHUNK = 64
DIL_GROUPS = ((128, 1), (512, 4), (2048, 16))
DIL_HEADS = 8
HEAD_DIM = 64
DIL_GW = DIL_HEADS * HEAD_DIM
DIL_W = len(DIL_GROUPS) * DIL_GW
DIL_BLOCK = 128
ROT_DIM = HEAD_DIM // 4
ROPE_THETA = 500000.0
LOG2_E = 1.4426950408889634
DIL_Q_SCALE = HEAD_DIM ** -0.5 * LOG2_E
STAT_LANES = 16
X_HEADS = 4
X_HEAD_DIM = D_MODEL // X_HEADS
D_FF = 2816
CONV_W = 3
IN_SIZES = (GLA_QK, GLA_QK, GLA_V, GLA_V, GLA_GATE_RANK, DIL_W, DIL_W, DIL_W)


def _proj_cols():
    cols, o = {}, 0
    for name, n in (("qa", GLA_QK), ("ka", GLA_QK), ("va", GLA_V), ("ra", GLA_V), ("qb", DIL_W), ("kb", DIL_W),
                    ("vb", DIL_W), ("za", 128)):
        cols[name] = (o, o + n)
        o += n
    return cols


PROJ_COLS = _proj_cols()

LANES = 128
VMEM_LIMIT = 56 * 1024 * 1024

TM_PROJ = 512
TS_GLA = 256
GLA_SEQS = 4
DIL_Q_BLOCKS = 4
TM_MERGE = 512
TM_XATTN = 1024
TM_FFN = 512
FF_CHUNK = D_FF
FFN_HALO = 8


def _params(sem):
    return pltpu.CompilerParams(dimension_semantics=sem, vmem_limit_bytes=VMEM_LIMIT)


def _resident(shape):
    nd = len(shape)
    return pl.BlockSpec(shape, lambda *_: (0,) * nd, pipeline_mode=pl.Buffered(1))


def _rms(x, gain):
    return x * lax.rsqrt(jnp.mean(x * x, axis=-1, keepdims=True) + EPS) * gain


def _store_residue_major(o_ref, col, val, scr_ref, dil):
    cs = slice(col, col + LANES)
    if dil == 1:
        o_ref[0, :, cs] = val.astype(BF16)
        return
    n = val.shape[0] // dil
    scr_ref[...] = val
    for r in range(dil):
        o_ref[r, :, cs] = scr_ref[pl.ds(r, n, stride=dil), :].astype(BF16)


def _inproj_kernel(x_ref, pos_ref, invf_ref, place_ref, nmix_ref, w_ref, wgg_ref, bgg_ref, qn_ref, kn_ref, bd_ref,
                   qa_ref, ka_ref, va_ref, ra_ref, la_ref, *rest):
    n_groups = len(DIL_GROUPS)
    q_refs, k_refs, v_refs = rest[:n_groups], rest[n_groups:2 * n_groups], rest[2 * n_groups:3 * n_groups]
    scratch = rest[3 * n_groups:]
    xn = _rms(x_ref[...], nmix_ref[...]).astype(BF16)

    def proj(name):
        lo, hi = PROJ_COLS[name]
        return jnp.dot(xn, w_ref[:, lo:hi], preferred_element_type=F32)

    ang = invf_ref[...] * pos_ref[...].astype(F32)
    terms = []
    for t in (jnp.cos(ang), jnp.sin(ang)):
        hi = t.astype(BF16)
        r1 = t - hi.astype(F32)
        mid = r1.astype(BF16)
        terms += [hi, mid, (r1 - mid.astype(F32)).astype(BF16)]
    tables = lax.dot_general(jnp.concatenate(terms, axis=0), place_ref[...], (((0,), (0,)), ((), ())),
                             preferred_element_type=F32)
    rot_lane = lax.broadcasted_iota(jnp.int32, (1, LANES), 1) & (HEAD_DIM - 1)
    cos = tables[:, :LANES] + jnp.where(rot_lane < ROT_DIM, 0.0, 1.0)
    sa, sb = tables[:, LANES:2 * LANES], tables[:, 2 * LANES:]
    bd = bd_ref[...]
    chunks_per_group = DIL_GW // LANES
    uses = [0]

    def emit(o_refs, j, val):
        gi = j // chunks_per_group
        scr = scratch[uses[0] % len(scratch)]
        uses[0] += 1
        _store_residue_major(o_refs[gi], (j % chunks_per_group) * LANES, val, scr, DIL_GROUPS[gi][1])

    def norm_rope(name, gain_ref, o_refs):
        y = proj(name)
        gain = gain_ref[...]
        for j in range(DIL_W // LANES):
            yc = y[:, j * LANES:(j + 1) * LANES]
            ss = jnp.dot((yc * yc).astype(BF16), bd, preferred_element_type=F32)
            yn = yc * lax.rsqrt(ss * (1.0 / HEAD_DIM) + EPS) * gain
            rot = yn * cos + pltpu.roll(yn, LANES - ROT_DIM // 2, axis=1) * sa + pltpu.roll(yn, ROT_DIM // 2, axis=1) * sb
            emit(o_refs, j, rot)

    norm_rope("qb", qn_ref, q_refs)
    norm_rope("kb", kn_ref, k_refs)
    yv = proj("vb")
    for j in range(DIL_W // LANES):
        emit(v_refs, j, yv[:, j * LANES:(j + 1) * LANES])

    za = proj("za").astype(BF16)
    gate = jnp.dot(za, wgg_ref[...], preferred_element_type=F32) + bgg_ref[...]
    la_ref[...] = jax.nn.log_sigmoid(gate) * (LOG2_E / GLA_TAU)
    ra_ref[...] = jax.nn.silu(proj("ra")).astype(BF16)
    qa_ref[...] = (proj("qa") * (GLA_DK ** -0.5)).astype(BF16)
    ka_ref[...] = proj("ka").astype(BF16)
    va_ref[...] = proj("va").astype(BF16)


def _win_kernel(w_ref, o_ref):
    gla_cols, rank = PROJ_COLS["ra"][1], GLA_GATE_RANK
    o_ref[:, :gla_cols] = w_ref[:, :gla_cols].astype(BF16)
    o_ref[:, gla_cols:PROJ_COLS["za"][0]] = w_ref[:, gla_cols + rank:].astype(BF16)
    gate = jnp.concatenate([w_ref[:, gla_cols:gla_cols + rank], jnp.zeros((w_ref.shape[0], LANES - rank), F32)], axis=1)
    o_ref[:, PROJ_COLS["za"][0]:] = gate.astype(BF16)


def _prepare_w_in(w_in):
    rows = 128
    n_out = PROJ_COLS["za"][1]
    return pl.pallas_call(
        _win_kernel, grid=(D_MODEL // rows,),
        in_specs=[pl.BlockSpec((rows, w_in.shape[1]), lambda i: (i, 0))],
        out_specs=pl.BlockSpec((rows, n_out), lambda i: (i, 0)),
        out_shape=jax.ShapeDtypeStruct((D_MODEL, n_out), BF16),
        compiler_params=_params(("parallel",)), name="prepare_w_in",
    )(w_in)


def _in_projection(x2, positions, batch, seq, norm_mix, w_in, w_gla_gate, b_gla_gate, dil_q_norm, dil_k_norm):
    tm = TM_PROJ
    ns = seq // tm
    w_all = _prepare_w_in(w_in)
    wgg = jnp.pad(w_gla_gate.astype(BF16), ((0, LANES - GLA_GATE_RANK), (0, 0)))
    lane = jnp.arange(LANES)
    bd = (lane[:, None] // HEAD_DIM == lane[None, :] // HEAD_DIM).astype(BF16)
    qn = jnp.tile(dil_q_norm * DIL_Q_SCALE, LANES // HEAD_DIM).reshape(1, LANES)
    kn = jnp.tile(dil_k_norm, LANES // HEAD_DIM).reshape(1, LANES)
    half = ROT_DIM // 2
    invf = (ROPE_THETA ** (-jnp.arange(0, ROT_DIM, 2, dtype=F32) / ROT_DIM)).reshape(half, 1)
    rot_lane = jnp.arange(LANES) % HEAD_DIM
    freq = jnp.arange(half)[:, None]
    cos_place = ((rot_lane[None, :] % half == freq) & (rot_lane[None, :] < ROT_DIM)).astype(F32)
    sa_place = -((rot_lane[None, :] == freq)).astype(F32)
    sb_place = ((rot_lane[None, :] == freq + half)).astype(F32)
    zero = jnp.zeros_like(cos_place)
    place = jnp.concatenate([jnp.tile(jnp.concatenate([cos_place, zero, zero], axis=1), (3, 1)),
                             jnp.tile(jnp.concatenate([zero, sa_place, sb_place], axis=1), (3, 1))], axis=0).astype(BF16)
    weights = [invf, place, norm_mix.reshape(1, D_MODEL), w_all, wgg,
               b_gla_gate.reshape(1, GLA_QK), qn, kn, bd]

    def row(w):
        return pl.BlockSpec((tm, w), lambda b, s: (b * ns + s, 0))

    t = batch * seq
    flat_w = [(GLA_QK, BF16), (GLA_QK, BF16), (GLA_V, BF16), (GLA_V, BF16), (GLA_QK, F32)]
    out_specs = [row(w) for w, _ in flat_w]
    out_shape = [jax.ShapeDtypeStruct((t, w), dt) for w, dt in flat_w]
    for _ in range(3):
        for _, dil in DIL_GROUPS:
            out_specs.append(pl.BlockSpec((None, dil, tm // dil, DIL_GW), lambda b, s: (b, 0, s, 0)))
            out_shape.append(jax.ShapeDtypeStruct((batch, dil, seq // dil, DIL_GW), BF16))
    outs = pl.pallas_call(
        _inproj_kernel, grid=(batch, ns),
        in_specs=[row(D_MODEL), pl.BlockSpec((None, 1, tm), lambda b, s: (b * ns + s, 0, 0))]
        + [_resident(w.shape) for w in weights],
        out_specs=out_specs, out_shape=out_shape,
        scratch_shapes=[pltpu.VMEM((tm, LANES), F32)] * 4,
        compiler_params=_params(("parallel", "parallel")), name="in_projection",
    )(x2, positions.reshape(batch * ns, 1, tm), *weights)
    n = len(DIL_GROUPS)
    return outs[:5], outs[5:5 + n], outs[5 + n:5 + 2 * n], outs[5 + 2 * n:]


def _gla_level_refs(g):
    c = GLA_CHUNK
    refs = []
    for h in (32, 16, 8):
        parts = [jnp.broadcast_to(g[m:m + 1, :], (2 * h, g.shape[1])) for m in range(h, c, 2 * h)]
        refs.append(parts[0] if len(parts) == 1 else jnp.concatenate(parts, axis=0))
    g3 = g.reshape(c // 8, 8, g.shape[1])
    sub = lax.broadcasted_iota(jnp.int32, g3.shape, 1)

    def pick(s):
        return jnp.broadcast_to(g3[:, s:s + 1, :], g3.shape)

    refs.append(pick(4).reshape(g.shape))
    refs.append(jnp.where(sub < 4, pick(2), pick(6)).reshape(g.shape))
    r1 = jnp.where(sub < 2, pick(1), jnp.where(sub < 4, pick(3), jnp.where(sub < 6, pick(5), pick(7))))
    refs.append(r1.reshape(g.shape))
    return refs


def _gla_pair_masks():
    c = GLA_CHUNK
    ri = jnp.arange(c)[:, None]
    ci = jnp.arange(c)[None, :]
    slabs = [((ri >> (b + 1)) == (ci >> (b + 1))) & (((ri >> b) & 1) == 1) & (((ci >> b) & 1) == 0)
             for b in (5, 4, 3, 2, 1, 0)]
    slabs.append(ri == ci)
    return jnp.stack(slabs).astype(F32)


def _gla_kernel(qa_ref, ka_ref, va_ref, ra_ref, la_ref, gn_ref, pm_ref, o_ref, state_ref):
    c = GLA_CHUNK
    nb = qa_ref.shape[0]

    @pl.when(pl.program_id(1) == 0)
    def _():
        state_ref[...] = jnp.zeros_like(state_ref)

    ri = lax.broadcasted_iota(jnp.int32, (c, c), 0)
    ci = lax.broadcasted_iota(jnp.int32, (c, c), 1)
    tril = (ri >= ci).astype(BF16)
    row = lax.broadcasted_iota(jnp.int32, (c, GLA_DK), 0)
    upper = [((row >> b) & 1) == 1 for b in (5, 4, 3, 2, 1, 0)]
    gn = gn_ref[...]

    def chunk(ic, carry):
        rows = pl.ds(pl.multiple_of(ic * c, c), c)
        items = [(bb, h) for bb in range(nb) for h in range(GLA_HEADS)]
        kcol = lambda h: slice(h * GLA_DK, (h + 1) * GLA_DK)
        vcol = lambda h: slice(h * GLA_DV, (h + 1) * GLA_DV)
        nt = (((1,), (1,)), ((), ()))
        g_all = []
        for bb in range(nb):
            la = la_ref[bb, rows, :]
            la_hi = la.astype(BF16)
            la_lo = (la - la_hi.astype(F32)).astype(BF16)
            g_all.append(jnp.dot(tril, la_hi, preferred_element_type=F32)
                         + jnp.dot(tril, la_lo, preferred_element_type=F32))
        qs = [qa_ref[bb, rows, kcol(h)].astype(F32) for bb, h in items]
        ks_ = [ka_ref[bb, rows, kcol(h)].astype(F32) for bb, h in items]
        vs_ = [va_ref[bb, rows, vcol(h)] for bb, h in items]
        gs = [g_all[bb][:, kcol(h)] for bb, h in items]
        sts = [state_ref[bb, h] for bb, h in items]
        n = range(len(items))
        o_inter = [lax.dot_general((qs[i] * jnp.exp2(gs[i])).astype(BF16), sts[i].astype(BF16), nt,
                                   preferred_element_type=F32) for i in n]
        grams, diags = [], []
        for i in n:
            zs = []
            for up, gm in zip(upper, _gla_level_refs(gs[i])):
                zs.append((jnp.where(up, qs[i], ks_[i]) * jnp.exp2(-jnp.abs(gs[i] - gm))).astype(BF16))
            z = jnp.stack(zs)
            grams.append(jnp.einsum('lik,ljk->lij', z, z, preferred_element_type=F32))
            diags.append(lax.dot_general(qs[i].astype(BF16), ks_[i].astype(BF16), nt, preferred_element_type=F32))
        n_levels = len(upper)
        attn = []
        for i in n:
            a = pm_ref[n_levels] * diags[i]
            for l in range(n_levels):
                a = a + pm_ref[l] * grams[i][l]
            attn.append(a.astype(BF16))
        outs = [o_inter[i] + jnp.dot(attn[i], vs_[i], preferred_element_type=F32) for i in n]
        for i, (bb, h) in enumerate(items):
            g_last = gs[i][c - 1:c, :]
            kd = (ks_[i] * jnp.exp2(g_last - gs[i])).astype(BF16)
            state_ref[bb, h] = sts[i] * jnp.exp2(g_last) + lax.dot_general(
                vs_[i], kd, (((0,), (0,)), ((), ())), preferred_element_type=F32)
        for i, (bb, h) in enumerate(items):
            o = _rms(outs[i], gn)
            o_ref[bb, rows, vcol(h)] = (o * ra_ref[bb, rows, vcol(h)].astype(F32)).astype(BF16)
        return carry

    lax.fori_loop(0, qa_ref.shape[1] // c, chunk, 0)


def _gla(qa, ka, va, ra, la, gla_out_norm, batch, seq):
    ts = TS_GLA
    nb = GLA_SEQS
    assert batch % nb == 0 and seq % ts == 0

    def view(a):
        return a.reshape(batch, seq, a.shape[-1])

    def blk(w):
        return pl.BlockSpec((nb, ts, w), lambda b, s: (b, s, 0))

    out = pl.pallas_call(
        _gla_kernel, grid=(batch // nb, seq // ts),
        in_specs=[blk(GLA_QK), blk(GLA_QK), blk(GLA_V), blk(GLA_V), blk(GLA_QK),
                  _resident((1, GLA_DV)), _resident((7, GLA_CHUNK, GLA_CHUNK))],
        out_specs=blk(GLA_V),
        out_shape=jax.ShapeDtypeStruct((batch, seq, GLA_V), BF16),
        scratch_shapes=[pltpu.VMEM((nb, GLA_HEADS, GLA_DV, GLA_DK), F32)],
        compiler_params=_params(("parallel", "arbitrary")), name="gla",
    )(view(qa), view(ka), view(va), view(ra), view(la), gla_out_norm.reshape(1, GLA_DV), _gla_pair_masks())
    return out.reshape(batch * seq, GLA_V)


def _dil_kernel(q_ref, kp_ref, kc_ref, vp_ref, vc_ref, acc_ref, stat_ref):
    blk = DIL_BLOCK
    has_prev = pl.program_id(2) > 0
    qi = lax.broadcasted_iota(jnp.int32, (blk, 2 * blk), 0)
    kj = lax.broadcasted_iota(jnp.int32, (blk, 2 * blk), 1)
    dist = qi + blk - kj
    band = (dist >= 0) & (dist <= blk)
    band_first = band & ((kj >= blk) | has_prev)
    lane = lax.broadcasted_iota(jnp.int32, (blk, LANES), 1)
    nt = (((1,), (1,)), ((), ()))
    pairs = DIL_GW // LANES
    for res, sb in [(res, sb) for res in range(q_ref.shape[0]) for sb in range(q_ref.shape[1] // blk)]:
        rows = slice(sb * blk, (sb + 1) * blk)
        before = slice((sb - 1) * blk, sb * blk)
        mask = band_first if sb == 0 else band
        scores, values = [], []
        for p in range(pairs):
            cs = slice(p * LANES, (p + 1) * LANES)
            q2 = q_ref[res, rows, cs]
            kp, vp = ((kp_ref[res, :, cs], vp_ref[res, :, cs]) if sb == 0
                      else (kc_ref[res, before, cs], vc_ref[res, before, cs]))
            kk = jnp.concatenate([kp, kc_ref[res, rows, cs]], axis=0)
            values.append(jnp.concatenate([vp, vc_ref[res, rows, cs]], axis=0))
            for hh in range(LANES // HEAD_DIM):
                mine = (lane < HEAD_DIM) if hh == 0 else (lane >= HEAD_DIM)
                qm = jnp.where(mine, q2, jnp.zeros_like(q2))
                scores.append(lax.dot_general(qm, kk, nt, preferred_element_type=F32))
        probs = []
        for idx, s in enumerate(scores):
            s = jnp.where(mask, s, -jnp.inf)
            m = jnp.max(s, axis=-1, keepdims=True)
            e = jnp.exp2(s - m)
            l = jnp.sum(e, axis=-1, keepdims=True)
            probs.append(e.astype(BF16))
            base, half = idx * STAT_LANES, STAT_LANES // 2
            stat_ref[res, rows, base:base + half] = jnp.broadcast_to(m, (blk, half))
            stat_ref[res, rows, base + half:base + STAT_LANES] = jnp.broadcast_to(l, (blk, half))
        for p in range(pairs):
            acc0 = jnp.dot(probs[2 * p], values[p], preferred_element_type=F32)
            acc1 = jnp.dot(probs[2 * p + 1], values[p], preferred_element_type=F32)
            acc_ref[res, rows, p * LANES:(p + 1) * LANES] = jnp.where(lane < HEAD_DIM, acc0, acc1).astype(BF16)


def _dilated_group(q, k, v):
    batch, dil, length, _ = q.shape
    per_step = min(DIL_Q_BLOCKS, length // DIL_BLOCK)
    steps = length // (per_step * DIL_BLOCK)
    residues = DIL_Q_BLOCKS // per_step

    def cur(w):
        return pl.BlockSpec((None, residues, per_step * DIL_BLOCK, w), lambda b, r, n: (b, r, n, 0))

    prev = pl.BlockSpec((None, residues, DIL_BLOCK, DIL_GW), lambda b, r, n: (b, r, jnp.maximum(per_step * n - 1, 0), 0))
    stat = jax.ShapeDtypeStruct((batch, dil, length, LANES), F32)
    return pl.pallas_call(
        _dil_kernel, grid=(batch, dil // residues, steps),
        in_specs=[cur(DIL_GW), prev, cur(DIL_GW), prev, cur(DIL_GW)],
        out_specs=[cur(DIL_GW), cur(LANES)],
        out_shape=[jax.ShapeDtypeStruct(q.shape, BF16), stat],
        compiler_params=_params(("parallel", "parallel", "arbitrary")), name=f"dilated_attn_d{dil}",
    )(q, k, k, v, v)


def _load_token_major(src_ref, col, scr_ref, dil):
    cs = slice(col, col + LANES)
    if dil == 1:
        return src_ref[0, :, cs].astype(F32)
    n = src_ref.shape[1]
    for r in range(dil):
        scr_ref[pl.ds(r, n, stride=dil), :] = src_ref[r, :, cs].astype(F32)
    return scr_ref[...]


def _merge_kernel(x_ref, oa_ref, a0_ref, a1_ref, a2_ref, s0_ref, s1_ref, s2_ref, nmix_ref, wmg_ref, bmg_ref, wbg_ref,
                  wbd_ref, wmo_ref, h_ref, *scratch):
    x = x_ref[...]
    xn = _rms(x, nmix_ref[...]).astype(BF16)
    gates = jax.nn.sigmoid(jnp.dot(xn, wmg_ref[...], preferred_element_type=F32) + bmg_ref[...])
    br_a = jnp.dot(oa_ref[...], wbg_ref[...], preferred_element_type=F32)
    a_refs, s_refs = (a0_ref, a1_ref, a2_ref), (s0_ref, s1_ref, s2_ref)
    dils = [dil for _, dil in DIL_GROUPS]
    uses = [0]

    def token_major(ref, col, dil):
        uses[0] += 1
        return _load_token_major(ref, col, scratch[uses[0] % len(scratch)], dil)

    ms = [token_major(r, 0, d) for r, d in zip(s_refs, dils)]
    ls = [pltpu.roll(st, LANES - STAT_LANES // 2, axis=1) for st in ms]
    top = jnp.maximum(jnp.maximum(ms[0], ms[1]), ms[2])
    es = [jnp.exp2(m - top) for m in ms]
    inv = 1.0 / (es[0] * ls[0] + es[1] * ls[1] + es[2] * ls[2])
    def widen(w):
        cols = [jnp.broadcast_to(w[:, STAT_LANES * hd:STAT_LANES * hd + 1], (w.shape[0], HEAD_DIM))
                for hd in range(DIL_HEADS)]
        return jnp.concatenate(cols, axis=1)

    wide = [widen(e * inv) for e in es]
    mixed = []
    for j in range(DIL_GW // LANES):
        cs = slice(j * LANES, (j + 1) * LANES)
        mixed.append(sum(w[:, cs] * token_major(r, j * LANES, d) for w, r, d in zip(wide, a_refs, dils)).astype(BF16))
    br_b = jnp.dot(jnp.concatenate(mixed, axis=1), wbd_ref[...], preferred_element_type=F32)
    merged = gates[:, :D_MODEL] * br_a + gates[:, D_MODEL:] * br_b
    h_ref[...] = x + jnp.dot(merged.astype(BF16), wmo_ref[...], preferred_element_type=F32)


def _merge(x2, oa, accs, stats, batch, seq, norm_mix, w_merge_gate, b_merge_gate, w_br_gla, w_br_dil, w_mix_out):
    tm = TM_MERGE
    ns = seq // tm
    weights = [norm_mix.reshape(1, D_MODEL), w_merge_gate.astype(BF16), b_merge_gate.reshape(1, 2 * D_MODEL),
               w_br_gla.astype(BF16), w_br_dil.astype(BF16), w_mix_out.astype(BF16)]

    def row(w):
        return pl.BlockSpec((tm, w), lambda b, s: (b * ns + s, 0))

    def grouped(w):
        return [pl.BlockSpec((None, dil, tm // dil, w), lambda b, s: (b, 0, s, 0)) for _, dil in DIL_GROUPS]

    return pl.pallas_call(
        _merge_kernel, grid=(batch, ns),
        in_specs=[row(D_MODEL), row(GLA_V)] + grouped(DIL_GW) + grouped(LANES)
        + [_resident(w.shape) for w in weights],
        out_specs=row(D_MODEL), out_shape=jax.ShapeDtypeStruct((batch * seq, D_MODEL), F32),
        scratch_shapes=[pltpu.VMEM((tm, LANES), F32)] * 4,
        compiler_params=_params(("parallel", "parallel")), name="merge",
    )(x2, oa, *accs, *stats, *weights)


def _memkv_kernel(mem_ref, nmem_ref, wkv_ref, kn_ref, k_ref, v_ref):
    mn = _rms(mem_ref[...], nmem_ref[...]).astype(BF16)
    kv = jnp.dot(mn, wkv_ref[...], preferred_element_type=F32)
    kn = kn_ref[...]
    for h in range(X_HEADS):
        cs = slice(h * X_HEAD_DIM, (h + 1) * X_HEAD_DIM)
        k_ref[:, cs] = _rms(kv[:, cs], kn).astype(BF16)
    v_ref[...] = kv[:, D_MODEL:].astype(BF16)


def _mem_kv(mem, norm_mem, w_xkv, x_k_norm):
    batch, n_mem, _ = mem.shape
    blk = pl.BlockSpec((None, n_mem, D_MODEL), lambda b: (b, 0, 0))
    weights = [norm_mem.reshape(1, D_MODEL), w_xkv.astype(BF16), x_k_norm.reshape(1, X_HEAD_DIM)]
    out = jax.ShapeDtypeStruct((batch, n_mem, D_MODEL), BF16)
    return pl.pallas_call(
        _memkv_kernel, grid=(batch,),
        in_specs=[blk] + [_resident(w.shape) for w in weights],
        out_specs=[blk, blk], out_shape=[out, out],
        compiler_params=_params(("parallel",)), name="mem_kv",
    )(mem, *weights)


def _xattn_kernel(h_ref, k_ref, v_ref, nx_ref, wq_ref, qn_ref, wo_ref, o_ref):
    h = h_ref[...]
    xn = _rms(h, nx_ref[...]).astype(BF16)
    q = jnp.dot(xn, wq_ref[...], preferred_element_type=F32)
    qn = qn_ref[...]
    cols = [slice(hd * X_HEAD_DIM, (hd + 1) * X_HEAD_DIM) for hd in range(X_HEADS)]
    nt = (((1,), (1,)), ((), ()))
    scores = [lax.dot_general((_rms(q[:, cs], qn) * (X_HEAD_DIM ** -0.5 * LOG2_E)).astype(BF16), k_ref[:, cs], nt,
                              preferred_element_type=F32) for cs in cols]
    probs, inv = [], []
    for s in scores:
        e = jnp.exp2(s - jnp.max(s, axis=-1, keepdims=True))
        inv.append(1.0 / jnp.sum(e, axis=-1, keepdims=True))
        probs.append(e.astype(BF16))
    outs = [(jnp.dot(p, v_ref[:, cs], preferred_element_type=F32) * r).astype(BF16)
            for p, r, cs in zip(probs, inv, cols)]
    o = jnp.concatenate(outs, axis=-1)
    o_ref[...] = h + jnp.dot(o, wo_ref[...], preferred_element_type=F32)


def _cross_attention(h1, kmem, vmem, norm_x, w_xq, x_q_norm, w_xo, batch, seq):
    tm = TM_XATTN
    ns = seq // tm
    n_mem = kmem.shape[1]
    weights_a = [norm_x.reshape(1, D_MODEL), w_xq.astype(BF16), x_q_norm.reshape(1, X_HEAD_DIM), w_xo.astype(BF16)]
    row = pl.BlockSpec((tm, D_MODEL), lambda i: (i, 0))
    memblk = pl.BlockSpec((None, n_mem, D_MODEL), lambda i: (i // ns, 0, 0))
    return pl.pallas_call(
        _xattn_kernel, grid=(batch * ns,),
        in_specs=[row, memblk, memblk] + [_resident(w.shape) for w in weights_a],
        out_specs=row, out_shape=jax.ShapeDtypeStruct(h1.shape, F32),
        compiler_params=_params(("parallel",)), name="cross_attention",
    )(h1, kmem, vmem, *weights_a)


def _ffn_kernel(h_ref, nf_ref, wup_ref, wc_ref, bc_ref, wd_ref, o_ref, halo_ref, *stage_refs):
    tm = h_ref.shape[0]
    pad = FFN_HALO

    @pl.when(pl.program_id(1) == 0)
    def _():
        halo_ref[...] = jnp.zeros_like(halo_ref)

    h = h_ref[...]
    xn = _rms(h, nf_ref[...]).astype(BF16)
    acc = jnp.zeros((tm, D_MODEL), F32)
    for j in range(D_FF // FF_CHUNK):
        cs = slice(j * FF_CHUNK, (j + 1) * FF_CHUNK)
        stage = stage_refs[j % len(stage_refs)]
        a = jnp.dot(xn, wup_ref[:, cs], preferred_element_type=F32)
        u = jnp.dot(xn, wup_ref[:, D_FF + j * FF_CHUNK:D_FF + (j + 1) * FF_CHUNK], preferred_element_type=F32)
        stage[0:pad, :] = halo_ref[:, cs]
        stage[pad:pad + tm, :] = a
        halo_ref[:, cs] = a[tm - pad:tm, :]
        conv = bc_ref[:, cs] + a * wc_ref[2:3, cs]
        conv = conv + stage[pad - 1:pad - 1 + tm, :] * wc_ref[1:2, cs]
        conv = conv + stage[pad - 2:pad - 2 + tm, :] * wc_ref[0:1, cs]
        gelu = 0.5 * conv * (1.0 + lax.erf(conv * (2.0 ** -0.5)))
        acc = acc + jnp.dot((gelu * u).astype(BF16), wd_ref[cs, :], preferred_element_type=F32)
    o_ref[...] = h + acc


def _ffn(h2, norm_ffn, w_ffn_up, w_ffn_conv, b_ffn_conv, w_ffn_down, batch, seq):
    tm = TM_FFN
    ns = seq // tm
    weights = [norm_ffn.reshape(1, D_MODEL), w_ffn_up.astype(BF16),
               w_ffn_conv, b_ffn_conv.reshape(1, D_FF), w_ffn_down.astype(BF16)]
    row = pl.BlockSpec((tm, D_MODEL), lambda b, s: (b * ns + s, 0))
    return pl.pallas_call(
        _ffn_kernel, grid=(batch, ns),
        in_specs=[row] + [_resident(w.shape) for w in weights],
        out_specs=row, out_shape=jax.ShapeDtypeStruct(h2.shape, F32),
        scratch_shapes=[pltpu.VMEM((FFN_HALO, D_FF), F32)]
        + [pltpu.VMEM((tm + FFN_HALO, FF_CHUNK), F32)] * min(2, D_FF // FF_CHUNK),
        compiler_params=_params(("parallel", "arbitrary")), name="conv_glu_ffn",
    )(h2, *weights)


def _layer(h, positions, mem, batch, seq, norm_mix, w_in, w_gla_gate, b_gla_gate, gla_out_norm, dil_q_norm, dil_k_norm,
           w_br_gla, w_br_dil, w_merge_gate, b_merge_gate, w_mix_out, norm_x, norm_mem, w_xq, w_xkv, x_q_norm,
           x_k_norm, w_xo, norm_ffn, w_ffn_up, w_ffn_conv, b_ffn_conv, w_ffn_down):
    (qa, ka, va, ra, la), qs, ks, vs = _in_projection(h, positions, batch, seq, norm_mix, w_in, w_gla_gate, b_gla_gate,
                                                      dil_q_norm, dil_k_norm)
    oa = _gla(qa, ka, va, ra, la, gla_out_norm, batch, seq)
    accs, stats = zip(*(_dilated_group(q, k, v) for q, k, v in zip(qs, ks, vs)))
    h1 = _merge(h, oa, accs, stats, batch, seq, norm_mix, w_merge_gate, b_merge_gate, w_br_gla, w_br_dil, w_mix_out)
    kmem, vmem = _mem_kv(mem, norm_mem, w_xkv, x_k_norm)
    h2 = _cross_attention(h1, kmem, vmem, norm_x, w_xq, x_q_norm, w_xo, batch, seq)
    return _ffn(h2, norm_ffn, w_ffn_up, w_ffn_conv, b_ffn_conv, w_ffn_down, batch, seq)


def kernel(x, mem, positions, norm_mix, w_in, w_gla_gate, b_gla_gate, gla_out_norm, dil_q_norm, dil_k_norm, w_br_gla, w_br_dil, w_merge_gate, b_merge_gate, w_mix_out, norm_x, norm_mem, w_xq, w_xkv, x_q_norm, x_k_norm, w_xo, norm_ffn, w_ffn_up, w_ffn_conv, b_ffn_conv, w_ffn_down):
    batch, seq, _ = x.shape
    h = x.reshape(batch * seq, D_MODEL)
    stacked = (norm_mix, w_in, w_gla_gate, b_gla_gate, gla_out_norm, dil_q_norm, dil_k_norm, w_br_gla, w_br_dil,
               w_merge_gate, b_merge_gate, w_mix_out, norm_x, norm_mem, w_xq, w_xkv, x_q_norm, x_k_norm, w_xo,
               norm_ffn, w_ffn_up, w_ffn_conv, b_ffn_conv, w_ffn_down)
    for l in range(norm_mix.shape[0]):
        h = _layer(h, positions, mem, batch, seq, *(p[l] for p in stacked))
    return h.reshape(batch, seq, D_MODEL)
```

```python
import functools

import jax
import jax.numpy as jnp
from jax import lax
from jax.experimental import pallas as pl
from jax.experimental.pallas import tpu as pltpu

F32 = jnp.float32
BF16 = jnp.bfloat16

D_MODEL = 1024
EPS = 1e-6
GLA_HEADS = 4
GLA_DK = 128
GLA_DV = 256
GLA_QK = GLA_HEADS * GLA_DK
GLA_V = GLA_HEADS * GLA_DV
GLA_GATE_RANK = 16
GLA_TAU = 16.0
GLA_CHUNK = 64
DIL_GROUPS = ((128, 1), (512, 4), (2048, 16))
DIL_HEADS = 8
HEAD_DIM = 64
DIL_GW = DIL_HEADS * HEAD_DIM
DIL_W = len(DIL_GROUPS) * DIL_GW
DIL_BLOCK = 128
ROT_DIM = HEAD_DIM // 4
ROPE_THETA = 500000.0
LOG2_E = 1.4426950408889634
DIL_Q_SCALE = HEAD_DIM ** -0.5 * LOG2_E
STAT_LANES = 16
X_HEADS = 4
X_HEAD_DIM = D_MODEL // X_HEADS
D_FF = 2816
CONV_W = 3
IN_SIZES = (GLA_QK, GLA_QK, GLA_V, GLA_V, GLA_GATE_RANK, DIL_W, DIL_W, DIL_W)


def _proj_cols():
    cols, o = {}, 0
    for name, n in (("qa", GLA_QK), ("ka", GLA_QK), ("va", GLA_V), ("ra", GLA_V), ("qb", DIL_W), ("kb", DIL_W),
                    ("vb", DIL_W), ("za", 128)):
        cols[name] = (o, o + n)
        o += n
    return cols


PROJ_COLS = _proj_cols()

LANES = 128
VMEM_LIMIT = 56 * 1024 * 1024

TM_PROJ = 512
TS_GLA = 256
GLA_SEQS = 4
DIL_Q_BLOCKS = 8
TM_MERGE = 512
TM_XATTN = 1024
TM_FFN = 512
FF_CHUNK = D_FF
FFN_HALO = 8


def _params(sem):
    return pltpu.CompilerParams(dimension_semantics=sem, vmem_limit_bytes=VMEM_LIMIT)


def _resident(shape):
    nd = len(shape)
    return pl.BlockSpec(shape, lambda *_: (0,) * nd, pipeline_mode=pl.Buffered(1))


def _rms(x, gain):
    return x * lax.rsqrt(jnp.mean(x * x, axis=-1, keepdims=True) + EPS) * gain


def _store_residue_major(o_ref, col, val, scr_ref, dil):
    cs = slice(col, col + LANES)
    if dil == 1:
        o_ref[0, :, cs] = val.astype(BF16)
        return
    n = val.shape[0] // dil
    scr_ref[...] = val
    for r in range(dil):
        o_ref[r, :, cs] = scr_ref[pl.ds(r, n, stride=dil), :].astype(BF16)


def _inproj_kernel(x_ref, pos_ref, invf_ref, place_ref, nmix_ref, w_ref, wgg_ref, bgg_ref, qn_ref, kn_ref, bd_ref,
                   qa_ref, ka_ref, va_ref, ra_ref, la_ref, *rest):
    n_groups = len(DIL_GROUPS)
    q_refs, k_refs, v_refs = rest[:n_groups], rest[n_groups:2 * n_groups], rest[2 * n_groups:3 * n_groups]
    scratch = rest[3 * n_groups:]
    xn = _rms(x_ref[...], nmix_ref[...]).astype(BF16)

    def proj(name):
        lo, hi = PROJ_COLS[name]
        return jnp.dot(xn, w_ref[:, lo:hi], preferred_element_type=F32)

    ang = invf_ref[...] * pos_ref[...].astype(F32)
    terms = []
    for t in (jnp.cos(ang), jnp.sin(ang)):
        hi = t.astype(BF16)
        r1 = t - hi.astype(F32)
        mid = r1.astype(BF16)
        terms += [hi, mid, (r1 - mid.astype(F32)).astype(BF16)]
    tables = lax.dot_general(jnp.concatenate(terms, axis=0), place_ref[...], (((0,), (0,)), ((), ())),
                             preferred_element_type=F32)
    rot_lane = lax.broadcasted_iota(jnp.int32, (1, LANES), 1) & (HEAD_DIM - 1)
    cos = tables[:, :LANES] + jnp.where(rot_lane < ROT_DIM, 0.0, 1.0)
    sa, sb = tables[:, LANES:2 * LANES], tables[:, 2 * LANES:]
    bd = bd_ref[...]
    chunks_per_group = DIL_GW // LANES
    uses = [0]

    def emit(o_refs, j, val):
        gi = j // chunks_per_group
        scr = scratch[uses[0] % len(scratch)]
        uses[0] += 1
        _store_residue_major(o_refs[gi], (j % chunks_per_group) * LANES, val, scr, DIL_GROUPS[gi][1])

    def norm_rope(name, gain_ref, o_refs):
        y = proj(name)
        gain = gain_ref[...]
        for j in range(DIL_W // LANES):
            yc = y[:, j * LANES:(j + 1) * LANES]
            ss = jnp.dot((yc * yc).astype(BF16), bd, preferred_element_type=F32)
            yn = yc * lax.rsqrt(ss * (1.0 / HEAD_DIM) + EPS) * gain
            rot = yn * cos + pltpu.roll(yn, LANES - ROT_DIM // 2, axis=1) * sa + pltpu.roll(yn, ROT_DIM // 2, axis=1) * sb
            emit(o_refs, j, rot)

    norm_rope("qb", qn_ref, q_refs)
    norm_rope("kb", kn_ref, k_refs)
    yv = proj("vb")
    for j in range(DIL_W // LANES):
        emit(v_refs, j, yv[:, j * LANES:(j + 1) * LANES])

    za = proj("za").astype(BF16)
    gate = jnp.dot(za, wgg_ref[...], preferred_element_type=F32) + bgg_ref[...]
    la_ref[...] = jax.nn.log_sigmoid(gate) * (LOG2_E / GLA_TAU)
    ra_ref[...] = jax.nn.silu(proj("ra")).astype(BF16)
    qa_ref[...] = (proj("qa") * (GLA_DK ** -0.5)).astype(BF16)
    ka_ref[...] = proj("ka").astype(BF16)
    va_ref[...] = proj("va").astype(BF16)


def _win_kernel(w_ref, o_ref):
    gla_cols, rank = PROJ_COLS["ra"][1], GLA_GATE_RANK
    o_ref[:, :gla_cols] = w_ref[:, :gla_cols].astype(BF16)
    o_ref[:, gla_cols:PROJ_COLS["za"][0]] = w_ref[:, gla_cols + rank:].astype(BF16)
    gate = jnp.concatenate([w_ref[:, gla_cols:gla_cols + rank], jnp.zeros((w_ref.shape[0], LANES - rank), F32)], axis=1)
    o_ref[:, PROJ_COLS["za"][0]:] = gate.astype(BF16)


def _prepare_w_in(w_in, layer):
    rows = 128
    n_out = PROJ_COLS["za"][1]
    return pl.pallas_call(
        _win_kernel, grid=(D_MODEL // rows,),
        in_specs=[pl.BlockSpec((None, rows, w_in.shape[2]), lambda i: (layer, i, 0))],
        out_specs=pl.BlockSpec((rows, n_out), lambda i: (i, 0)),
        out_shape=jax.ShapeDtypeStruct((D_MODEL, n_out), BF16),
        compiler_params=_params(("parallel",)), name="prepare_w_in",
    )(w_in)


def _in_projection(x2, positions, batch, seq, norm_mix, w_all, w_gla_gate, b_gla_gate, dil_q_norm, dil_k_norm):
    tm = TM_PROJ
    ns = seq // tm
    wgg = jnp.pad(w_gla_gate.astype(BF16), ((0, LANES - GLA_GATE_RANK), (0, 0)))
    lane = jnp.arange(LANES)
    bd = (lane[:, None] // HEAD_DIM == lane[None, :] // HEAD_DIM).astype(BF16)
    qn = jnp.tile(dil_q_norm * DIL_Q_SCALE, LANES // HEAD_DIM).reshape(1, LANES)
    kn = jnp.tile(dil_k_norm, LANES // HEAD_DIM).reshape(1, LANES)
    half = ROT_DIM // 2
    invf = (ROPE_THETA ** (-jnp.arange(0, ROT_DIM, 2, dtype=F32) / ROT_DIM)).reshape(half, 1)
    rot_lane = jnp.arange(LANES) % HEAD_DIM
    freq = jnp.arange(half)[:, None]
    cos_place = ((rot_lane[None, :] % half == freq) & (rot_lane[None, :] < ROT_DIM)).astype(F32)
    sa_place = -((rot_lane[None, :] == freq)).astype(F32)
    sb_place = ((rot_lane[None, :] == freq + half)).astype(F32)
    zero = jnp.zeros_like(cos_place)
    place = jnp.concatenate([jnp.tile(jnp.concatenate([cos_place, zero, zero], axis=1), (3, 1)),
                             jnp.tile(jnp.concatenate([zero, sa_place, sb_place], axis=1), (3, 1))], axis=0).astype(BF16)
    weights = [invf, place, norm_mix.reshape(1, D_MODEL), w_all, wgg,
               b_gla_gate.reshape(1, GLA_QK), qn, kn, bd]

    def row(w):
        return pl.BlockSpec((tm, w), lambda b, s: (b * ns + s, 0))

    t = batch * seq
    flat_w = [(GLA_QK, BF16), (GLA_QK, BF16), (GLA_V, BF16), (GLA_V, BF16), (GLA_QK, F32)]
    out_specs = [row(w) for w, _ in flat_w]
    out_shape = [jax.ShapeDtypeStruct((t, w), dt) for w, dt in flat_w]
    for _ in range(3):
        for _, dil in DIL_GROUPS:
            out_specs.append(pl.BlockSpec((None, dil, tm // dil, DIL_GW), lambda b, s: (b, 0, s, 0)))
            out_shape.append(jax.ShapeDtypeStruct((batch, dil, seq // dil, DIL_GW), BF16))
    outs = pl.pallas_call(
        _inproj_kernel, grid=(batch, ns),
        in_specs=[row(D_MODEL), pl.BlockSpec((None, 1, tm), lambda b, s: (b * ns + s, 0, 0))]
        + [_resident(w.shape) for w in weights],
        out_specs=out_specs, out_shape=out_shape,
        scratch_shapes=[pltpu.VMEM((tm, LANES), F32)] * 4,
        compiler_params=_params(("parallel", "parallel")), name="in_projection",
    )(x2, positions.reshape(batch * ns, 1, tm), *weights)
    n = len(DIL_GROUPS)
    return outs[:5], outs[5:5 + n], outs[5 + n:5 + 2 * n], outs[5 + 2 * n:]


def _gla_level_refs(g):
    c = GLA_CHUNK
    refs = []
    for h in (32, 16, 8):
        parts = [jnp.broadcast_to(g[m:m + 1, :], (2 * h, g.shape[1])) for m in range(h, c, 2 * h)]
        refs.append(parts[0] if len(parts) == 1 else jnp.concatenate(parts, axis=0))
    g3 = g.reshape(c // 8, 8, g.shape[1])
    sub = lax.broadcasted_iota(jnp.int32, g3.shape, 1)

    def pick(s):
        return jnp.broadcast_to(g3[:, s:s + 1, :], g3.shape)

    refs.append(pick(4).reshape(g.shape))
    refs.append(jnp.where(sub < 4, pick(2), pick(6)).reshape(g.shape))
    r1 = jnp.where(sub < 2, pick(1), jnp.where(sub < 4, pick(3), jnp.where(sub < 6, pick(5), pick(7))))
    refs.append(r1.reshape(g.shape))
    return refs


def _gla_pair_masks():
    c = GLA_CHUNK
    ri = jnp.arange(c)[:, None]
    ci = jnp.arange(c)[None, :]
    slabs = [((ri >> (b + 1)) == (ci >> (b + 1))) & (((ri >> b) & 1) == 1) & (((ci >> b) & 1) == 0)
             for b in (5, 4, 3, 2, 1, 0)]
    slabs.append(ri == ci)
    return jnp.stack(slabs).astype(F32)


def _gla_kernel(qa_ref, ka_ref, va_ref, ra_ref, la_ref, gn_ref, pm_ref, o_ref, state_ref):
    c = GLA_CHUNK
    nb = qa_ref.shape[0]

    @pl.when(pl.program_id(1) == 0)
    def _():
        state_ref[...] = jnp.zeros_like(state_ref)

    ri = lax.broadcasted_iota(jnp.int32, (c, c), 0)
    ci = lax.broadcasted_iota(jnp.int32, (c, c), 1)
    tril = (ri >= ci).astype(BF16)
    row = lax.broadcasted_iota(jnp.int32, (c, GLA_DK), 0)
    upper = [((row >> b) & 1) == 1 for b in (5, 4, 3, 2, 1, 0)]
    gn = gn_ref[...]

    def chunk(ic, carry):
        rows = pl.ds(pl.multiple_of(ic * c, c), c)
        items = [(bb, h) for bb in range(nb) for h in range(GLA_HEADS)]
        kcol = lambda h: slice(h * GLA_DK, (h + 1) * GLA_DK)
        vcol = lambda h: slice(h * GLA_DV, (h + 1) * GLA_DV)
        nt = (((1,), (1,)), ((), ()))
        g_all = []
        for bb in range(nb):
            la = la_ref[bb, rows, :]
            la_hi = la.astype(BF16)
            la_lo = (la - la_hi.astype(F32)).astype(BF16)
            g_all.append(jnp.dot(tril, la_hi, preferred_element_type=F32)
                         + jnp.dot(tril, la_lo, preferred_element_type=F32))
        qs = [qa_ref[bb, rows, kcol(h)].astype(F32) for bb, h in items]
        ks_ = [ka_ref[bb, rows, kcol(h)].astype(F32) for bb, h in items]
        vs_ = [va_ref[bb, rows, vcol(h)] for bb, h in items]
        gs = [g_all[bb][:, kcol(h)] for bb, h in items]
        sts = [state_ref[bb, h] for bb, h in items]
        n = range(len(items))
        o_inter = [lax.dot_general((qs[i] * jnp.exp2(gs[i])).astype(BF16), sts[i].astype(BF16), nt,
                                   preferred_element_type=F32) for i in n]
        grams, diags = [], []
        for i in n:
            zs = []
            for up, gm in zip(upper, _gla_level_refs(gs[i])):
                zs.append((jnp.where(up, qs[i], ks_[i]) * jnp.exp2(-jnp.abs(gs[i] - gm))).astype(BF16))
            z = jnp.stack(zs)
            grams.append(jnp.einsum('lik,ljk->lij', z, z, preferred_element_type=F32))
            diags.append(lax.dot_general(qs[i].astype(BF16), ks_[i].astype(BF16), nt, preferred_element_type=F32))
        n_levels = len(upper)
        attn = []
        for i in n:
            a = pm_ref[n_levels] * diags[i]
            for l in range(n_levels):
                a = a + pm_ref[l] * grams[i][l]
            attn.append(a.astype(BF16))
        outs = [o_inter[i] + jnp.dot(attn[i], vs_[i], preferred_element_type=F32) for i in n]
        for i, (bb, h) in enumerate(items):
            g_last = gs[i][c - 1:c, :]
            kd = (ks_[i] * jnp.exp2(g_last - gs[i])).astype(BF16)
            state_ref[bb, h] = sts[i] * jnp.exp2(g_last) + lax.dot_general(
                vs_[i], kd, (((0,), (0,)), ((), ())), preferred_element_type=F32)
        for i, (bb, h) in enumerate(items):
            o = _rms(outs[i], gn)
            o_ref[bb, rows, vcol(h)] = (o * ra_ref[bb, rows, vcol(h)].astype(F32)).astype(BF16)
        return carry

    lax.fori_loop(0, qa_ref.shape[1] // c, chunk, 0)


def _gla(qa, ka, va, ra, la, gla_out_norm, batch, seq):
    ts = TS_GLA
    nb = GLA_SEQS
    assert batch % nb == 0 and seq % ts == 0

    def view(a):
        return a.reshape(batch, seq, a.shape[-1])

    def blk(w):
        return pl.BlockSpec((nb, ts, w), lambda b, s: (b, s, 0))

    out = pl.pallas_call(
        _gla_kernel, grid=(batch // nb, seq // ts),
        in_specs=[blk(GLA_QK), blk(GLA_QK), blk(GLA_V), blk(GLA_V), blk(GLA_QK),
                  _resident((1, GLA_DV)), _resident((7, GLA_CHUNK, GLA_CHUNK))],
        out_specs=blk(GLA_V),
        out_shape=jax.ShapeDtypeStruct((batch, seq, GLA_V), BF16),
        scratch_shapes=[pltpu.VMEM((nb, GLA_HEADS, GLA_DV, GLA_DK), F32)],
        compiler_params=_params(("parallel", "arbitrary")), name="gla",
    )(view(qa), view(ka), view(va), view(ra), view(la), gla_out_norm.reshape(1, GLA_DV), _gla_pair_masks())
    return out.reshape(batch * seq, GLA_V)


def _dil_kernel(q_ref, kp_ref, kc_ref, vp_ref, vc_ref, acc_ref, stat_ref):
    blk = DIL_BLOCK
    has_prev = pl.program_id(2) > 0
    qi = lax.broadcasted_iota(jnp.int32, (blk, 2 * blk), 0)
    kj = lax.broadcasted_iota(jnp.int32, (blk, 2 * blk), 1)
    dist = qi + blk - kj
    band = (dist >= 0) & (dist <= blk)
    band_first = band & ((kj >= blk) | has_prev)
    lane = lax.broadcasted_iota(jnp.int32, (blk, LANES), 1)
    nt = (((1,), (1,)), ((), ()))
    pairs = DIL_GW // LANES
    for res, sb in [(res, sb) for res in range(q_ref.shape[0]) for sb in range(q_ref.shape[1] // blk)]:
        rows = slice(sb * blk, (sb + 1) * blk)
        before = slice((sb - 1) * blk, sb * blk)
        mask = band_first if sb == 0 else band
        scores, values = [], []
        for p in range(pairs):
            cs = slice(p * LANES, (p + 1) * LANES)
            q2 = q_ref[res, rows, cs]
            kp, vp = ((kp_ref[res, :, cs], vp_ref[res, :, cs]) if sb == 0
                      else (kc_ref[res, before, cs], vc_ref[res, before, cs]))
            kk = jnp.concatenate([kp, kc_ref[res, rows, cs]], axis=0)
            values.append(jnp.concatenate([vp, vc_ref[res, rows, cs]], axis=0))
            for hh in range(LANES // HEAD_DIM):
                mine = (lane < HEAD_DIM) if hh == 0 else (lane >= HEAD_DIM)
                qm = jnp.where(mine, q2, jnp.zeros_like(q2))
                scores.append(lax.dot_general(qm, kk, nt, preferred_element_type=F32))
        probs = []
        for idx, s in enumerate(scores):
            s = jnp.where(mask, s, -jnp.inf)
            m = jnp.max(s, axis=-1, keepdims=True)
            e = jnp.exp2(s - m)
            l = jnp.sum(e, axis=-1, keepdims=True)
            probs.append(e.astype(BF16))
            base, half = idx * STAT_LANES, STAT_LANES // 2
            stat_ref[res, rows, base:base + half] = jnp.broadcast_to(m, (blk, half))
            stat_ref[res, rows, base + half:base + STAT_LANES] = jnp.broadcast_to(l, (blk, half))
        for p in range(pairs):
            acc0 = jnp.dot(probs[2 * p], values[p], preferred_element_type=F32)
            acc1 = jnp.dot(probs[2 * p + 1], values[p], preferred_element_type=F32)
            acc_ref[res, rows, p * LANES:(p + 1) * LANES] = jnp.where(lane < HEAD_DIM, acc0, acc1).astype(BF16)


def _dilated_group(q, k, v):
    batch, dil, length, _ = q.shape
    per_step = min(DIL_Q_BLOCKS, length // DIL_BLOCK)
    steps = length // (per_step * DIL_BLOCK)
    residues = DIL_Q_BLOCKS // per_step

    def cur(w):
        return pl.BlockSpec((None, residues, per_step * DIL_BLOCK, w), lambda b, r, n: (b, r, n, 0))

    prev = pl.BlockSpec((None, residues, DIL_BLOCK, DIL_GW), lambda b, r, n: (b, r, jnp.maximum(per_step * n - 1, 0), 0))
    stat = jax.ShapeDtypeStruct((batch, dil, length, LANES), F32)
    return pl.pallas_call(
        _dil_kernel, grid=(batch, dil // residues, steps),
        in_specs=[cur(DIL_GW), prev, cur(DIL_GW), prev, cur(DIL_GW)],
        out_specs=[cur(DIL_GW), cur(LANES)],
        out_shape=[jax.ShapeDtypeStruct(q.shape, BF16), stat],
        compiler_params=_params(("parallel", "parallel", "arbitrary")), name=f"dilated_attn_d{dil}",
    )(q, k, k, v, v)


def _load_token_major(src_ref, col, scr_ref, dil):
    cs = slice(col, col + LANES)
    if dil == 1:
        return src_ref[0, :, cs].astype(F32)
    n = src_ref.shape[1]
    for r in range(dil):
        scr_ref[pl.ds(r, n, stride=dil), :] = src_ref[r, :, cs].astype(F32)
    return scr_ref[...]


def _merge_kernel(x_ref, oa_ref, a0_ref, a1_ref, a2_ref, s0_ref, s1_ref, s2_ref, nmix_ref, wmg_ref, bmg_ref, wbg_ref,
                  wbd_ref, wmo_ref, h_ref, *scratch):
    x = x_ref[...]
    xn = _rms(x, nmix_ref[...]).astype(BF16)
    gates = jax.nn.sigmoid(jnp.dot(xn, wmg_ref[...], preferred_element_type=F32) + bmg_ref[...])
    br_a = jnp.dot(oa_ref[...], wbg_ref[...], preferred_element_type=F32)
    a_refs, s_refs = (a0_ref, a1_ref, a2_ref), (s0_ref, s1_ref, s2_ref)
    dils = [dil for _, dil in DIL_GROUPS]
    uses = [0]

    def token_major(ref, col, dil):
        uses[0] += 1
        return _load_token_major(ref, col, scratch[uses[0] % len(scratch)], dil)

    ms = [token_major(r, 0, d) for r, d in zip(s_refs, dils)]
    ls = [pltpu.roll(st, LANES - STAT_LANES // 2, axis=1) for st in ms]
    top = jnp.maximum(jnp.maximum(ms[0], ms[1]), ms[2])
    es = [jnp.exp2(m - top) for m in ms]
    inv = 1.0 / (es[0] * ls[0] + es[1] * ls[1] + es[2] * ls[2])
    def widen(w):
        cols = [jnp.broadcast_to(w[:, STAT_LANES * hd:STAT_LANES * hd + 1], (w.shape[0], HEAD_DIM))
                for hd in range(DIL_HEADS)]
        return jnp.concatenate(cols, axis=1)

    wide = [widen(e * inv) for e in es]
    mixed = []
    for j in range(DIL_GW // LANES):
        cs = slice(j * LANES, (j + 1) * LANES)
        mixed.append(sum(w[:, cs] * token_major(r, j * LANES, d) for w, r, d in zip(wide, a_refs, dils)).astype(BF16))
    br_b = jnp.dot(jnp.concatenate(mixed, axis=1), wbd_ref[...], preferred_element_type=F32)
    merged = gates[:, :D_MODEL] * br_a + gates[:, D_MODEL:] * br_b
    h_ref[...] = x + jnp.dot(merged.astype(BF16), wmo_ref[...], preferred_element_type=F32)


def _merge(x2, oa, accs, stats, batch, seq, norm_mix, w_merge_gate, b_merge_gate, w_br_gla, w_br_dil, w_mix_out):
    tm = TM_MERGE
    ns = seq // tm
    weights = [norm_mix.reshape(1, D_MODEL), w_merge_gate.astype(BF16), b_merge_gate.reshape(1, 2 * D_MODEL),
               w_br_gla.astype(BF16), w_br_dil.astype(BF16), w_mix_out.astype(BF16)]

    def row(w):
        return pl.BlockSpec((tm, w), lambda b, s: (b * ns + s, 0))

    def grouped(w):
        return [pl.BlockSpec((None, dil, tm // dil, w), lambda b, s: (b, 0, s, 0)) for _, dil in DIL_GROUPS]

    return pl.pallas_call(
        _merge_kernel, grid=(batch, ns),
        in_specs=[row(D_MODEL), row(GLA_V)] + grouped(DIL_GW) + grouped(LANES)
        + [_resident(w.shape) for w in weights],
        out_specs=row(D_MODEL), out_shape=jax.ShapeDtypeStruct((batch * seq, D_MODEL), F32),
        scratch_shapes=[pltpu.VMEM((tm, LANES), F32)] * 4,
        compiler_params=_params(("parallel", "parallel")), name="merge",
    )(x2, oa, *accs, *stats, *weights)


def _memkv_kernel(mem_ref, nmem_ref, wkv_ref, kn_ref, k_ref, v_ref):
    mn = _rms(mem_ref[...], nmem_ref[...]).astype(BF16)
    kv = jnp.dot(mn, wkv_ref[...], preferred_element_type=F32)
    kn = kn_ref[...]
    for h in range(X_HEADS):
        cs = slice(h * X_HEAD_DIM, (h + 1) * X_HEAD_DIM)
        k_ref[:, cs] = _rms(kv[:, cs], kn).astype(BF16)
    v_ref[...] = kv[:, D_MODEL:].astype(BF16)


def _mem_kv(mem, norm_mem, w_xkv, x_k_norm):
    batch, n_mem, _ = mem.shape
    blk = pl.BlockSpec((None, n_mem, D_MODEL), lambda b: (b, 0, 0))
    weights = [norm_mem.reshape(1, D_MODEL), w_xkv.astype(BF16), x_k_norm.reshape(1, X_HEAD_DIM)]
    out = jax.ShapeDtypeStruct((batch, n_mem, D_MODEL), BF16)
    return pl.pallas_call(
        _memkv_kernel, grid=(batch,),
        in_specs=[blk] + [_resident(w.shape) for w in weights],
        out_specs=[blk, blk], out_shape=[out, out],
        compiler_params=_params(("parallel",)), name="mem_kv",
    )(mem, *weights)


def _xattn_kernel(h_ref, k_ref, v_ref, nx_ref, wq_ref, qn_ref, wo_ref, o_ref):
    h = h_ref[...]
    xn = _rms(h, nx_ref[...]).astype(BF16)
    q = jnp.dot(xn, wq_ref[...], preferred_element_type=F32)
    qn = qn_ref[...]
    cols = [slice(hd * X_HEAD_DIM, (hd + 1) * X_HEAD_DIM) for hd in range(X_HEADS)]
    nt = (((1,), (1,)), ((), ()))
    scores = [lax.dot_general((_rms(q[:, cs], qn) * (X_HEAD_DIM ** -0.5 * LOG2_E)).astype(BF16), k_ref[:, cs], nt,
                              preferred_element_type=F32) for cs in cols]
    probs, inv = [], []
    for s in scores:
        e = jnp.exp2(s - jnp.max(s, axis=-1, keepdims=True))
        inv.append(1.0 / jnp.sum(e, axis=-1, keepdims=True))
        probs.append(e.astype(BF16))
    outs = [(jnp.dot(p, v_ref[:, cs], preferred_element_type=F32) * r).astype(BF16)
            for p, r, cs in zip(probs, inv, cols)]
    o = jnp.concatenate(outs, axis=-1)
    o_ref[...] = h + jnp.dot(o, wo_ref[...], preferred_element_type=F32)


def _cross_attention(h1, kmem, vmem, norm_x, w_xq, x_q_norm, w_xo, batch, seq):
    tm = TM_XATTN
    ns = seq // tm
    n_mem = kmem.shape[1]
    weights_a = [norm_x.reshape(1, D_MODEL), w_xq.astype(BF16), x_q_norm.reshape(1, X_HEAD_DIM), w_xo.astype(BF16)]
    row = pl.BlockSpec((tm, D_MODEL), lambda i: (i, 0))
    memblk = pl.BlockSpec((None, n_mem, D_MODEL), lambda i: (i // ns, 0, 0))
    return pl.pallas_call(
        _xattn_kernel, grid=(batch * ns,),
        in_specs=[row, memblk, memblk] + [_resident(w.shape) for w in weights_a],
        out_specs=row, out_shape=jax.ShapeDtypeStruct(h1.shape, F32),
        compiler_params=_params(("parallel",)), name="cross_attention",
    )(h1, kmem, vmem, *weights_a)


def _ffn_kernel(h_ref, nf_ref, wup_ref, wc_ref, bc_ref, wd_ref, o_ref, halo_ref, *stage_refs):
    tm = h_ref.shape[0]
    pad = FFN_HALO

    @pl.when(pl.program_id(1) == 0)
    def _():
        halo_ref[...] = jnp.zeros_like(halo_ref)

    h = h_ref[...]
    xn = _rms(h, nf_ref[...]).astype(BF16)
    acc = jnp.zeros((tm, D_MODEL), F32)
    for j in range(D_FF // FF_CHUNK):
        cs = slice(j * FF_CHUNK, (j + 1) * FF_CHUNK)
        stage = stage_refs[j % len(stage_refs)]
        a = jnp.dot(xn, wup_ref[:, cs], preferred_element_type=F32)
        u = jnp.dot(xn, wup_ref[:, D_FF + j * FF_CHUNK:D_FF + (j + 1) * FF_CHUNK], preferred_element_type=F32)
        stage[0:pad, :] = halo_ref[:, cs]
        stage[pad:pad + tm, :] = a
        halo_ref[:, cs] = a[tm - pad:tm, :]
        conv = bc_ref[:, cs] + a * wc_ref[2:3, cs]
        conv = conv + stage[pad - 1:pad - 1 + tm, :] * wc_ref[1:2, cs]
        conv = conv + stage[pad - 2:pad - 2 + tm, :] * wc_ref[0:1, cs]
        gelu = 0.5 * conv * (1.0 + lax.erf(conv * (2.0 ** -0.5)))
        acc = acc + jnp.dot((gelu * u).astype(BF16), wd_ref[cs, :], preferred_element_type=F32)
    o_ref[...] = h + acc


def _ffn(h2, norm_ffn, w_ffn_up, w_ffn_conv, b_ffn_conv, w_ffn_down, batch, seq):
    tm = TM_FFN
    ns = seq // tm
    weights = [norm_ffn.reshape(1, D_MODEL), w_ffn_up.astype(BF16),
               w_ffn_conv, b_ffn_conv.reshape(1, D_FF), w_ffn_down.astype(BF16)]
    row = pl.BlockSpec((tm, D_MODEL), lambda b, s: (b * ns + s, 0))
    return pl.pallas_call(
        _ffn_kernel, grid=(batch, ns),
        in_specs=[row] + [_resident(w.shape) for w in weights],
        out_specs=row, out_shape=jax.ShapeDtypeStruct(h2.shape, F32),
        scratch_shapes=[pltpu.VMEM((FFN_HALO, D_FF), F32)]
        + [pltpu.VMEM((tm + FFN_HALO, FF_CHUNK), F32)] * min(2, D_FF // FF_CHUNK),
        compiler_params=_params(("parallel", "arbitrary")), name="conv_glu_ffn",
    )(h2, *weights)


def _layer(h, positions, mem, batch, seq, w_all, norm_mix, w_gla_gate, b_gla_gate, gla_out_norm, dil_q_norm, dil_k_norm,
           w_br_gla, w_br_dil, w_merge_gate, b_merge_gate, w_mix_out, norm_x, norm_mem, w_xq, w_xkv, x_q_norm,
           x_k_norm, w_xo, norm_ffn, w_ffn_up, w_ffn_conv, b_ffn_conv, w_ffn_down):
    (qa, ka, va, ra, la), qs, ks, vs = _in_projection(h, positions, batch, seq, norm_mix, w_all, w_gla_gate,
                                                      b_gla_gate, dil_q_norm, dil_k_norm)
    oa = _gla(qa, ka, va, ra, la, gla_out_norm, batch, seq)
    accs, stats = zip(*(_dilated_group(q, k, v) for q, k, v in zip(qs, ks, vs)))
    h1 = _merge(h, oa, accs, stats, batch, seq, norm_mix, w_merge_gate, b_merge_gate, w_br_gla, w_br_dil, w_mix_out)
    kmem, vmem = _mem_kv(mem, norm_mem, w_xkv, x_k_norm)
    h2 = _cross_attention(h1, kmem, vmem, norm_x, w_xq, x_q_norm, w_xo, batch, seq)
    return _ffn(h2, norm_ffn, w_ffn_up, w_ffn_conv, b_ffn_conv, w_ffn_down, batch, seq)


def kernel(x, mem, positions, norm_mix, w_in, w_gla_gate, b_gla_gate, gla_out_norm, dil_q_norm, dil_k_norm, w_br_gla, w_br_dil, w_merge_gate, b_merge_gate, w_mix_out, norm_x, norm_mem, w_xq, w_xkv, x_q_norm, x_k_norm, w_xo, norm_ffn, w_ffn_up, w_ffn_conv, b_ffn_conv, w_ffn_down):
    batch, seq, _ = x.shape
    h = x.reshape(batch * seq, D_MODEL)
    stacked = (norm_mix, w_gla_gate, b_gla_gate, gla_out_norm, dil_q_norm, dil_k_norm, w_br_gla, w_br_dil,
               w_merge_gate, b_merge_gate, w_mix_out, norm_x, norm_mem, w_xq, w_xkv, x_q_norm, x_k_norm, w_xo,
               norm_ffn, w_ffn_up, w_ffn_conv, b_ffn_conv, w_ffn_down)
    for l in range(norm_mix.shape[0]):
        h = _layer(h, positions, mem, batch, seq, _prepare_w_in(w_in, l), *(p[l] for p in stacked))
    return h.reshape(batch, seq, D_MODEL)
```

```python
import functools

import jax
import jax.numpy as jnp
from jax import lax
from jax.experimental import pallas as pl
from jax.experimental.pallas import tpu as pltpu

F32 = jnp.float32
BF16 = jnp.bfloat16

D_MODEL = 1024
EPS = 1e-6
GLA_HEADS = 4
GLA_DK = 128
GLA_DV = 256
GLA_QK = GLA_HEADS * GLA_DK
GLA_V = GLA_HEADS * GLA_DV
GLA_GATE_RANK = 16
GLA_TAU = 16.0
GLA_CHUNK = 64
DIL_GROUPS = ((128, 1), (512, 4), (2048, 16))
DIL_HEADS = 8
HEAD_DIM = 64
DIL_GW = DIL_HEADS * HEAD_DIM
DIL_W = len(DIL_GROUPS) * DIL_GW
DIL_BLOCK = 128
ROT_DIM = HEAD_DIM // 4
ROPE_THETA = 500000.0
LOG2_E = 1.4426950408889634
DIL_Q_SCALE = HEAD_DIM ** -0.5 * LOG2_E
STAT_LANES = 16
X_HEADS = 4
X_HEAD_DIM = D_MODEL // X_HEADS
D_FF = 2816
CONV_W = 3
IN_SIZES = (GLA_QK, GLA_QK, GLA_V, GLA_V, GLA_GATE_RANK, DIL_W, DIL_W, DIL_W)


def _proj_cols():
    cols, o = {}, 0
    for name, n in (("qa", GLA_QK), ("ka", GLA_QK), ("va", GLA_V), ("ra", GLA_V), ("qb", DIL_W), ("kb", DIL_W),
                    ("vb", DIL_W), ("za", 128)):
        cols[name] = (o, o + n)
        o += n
    return cols


PROJ_COLS = _proj_cols()

LANES = 128
VMEM_LIMIT = 56 * 1024 * 1024

TM_PROJ = 512
TS_GLA = 128
GLA_SEQS = 8
DIL_Q_BLOCKS = 16
TM_MERGE = 512
TM_XATTN = 1024
TM_FFN = 512
FF_CHUNK = D_FF
FFN_HALO = 8


def _params(sem):
    return pltpu.CompilerParams(dimension_semantics=sem, vmem_limit_bytes=VMEM_LIMIT)


def _resident(shape):
    nd = len(shape)
    return pl.BlockSpec(shape, lambda *_: (0,) * nd, pipeline_mode=pl.Buffered(1))


def _rms(x, gain):
    return x * lax.rsqrt(jnp.mean(x * x, axis=-1, keepdims=True) + EPS) * gain


def _store_residue_major(o_ref, col, val, scr_ref, dil):
    cs = slice(col, col + LANES)
    if dil == 1:
        o_ref[0, :, cs] = val.astype(BF16)
        return
    n = val.shape[0] // dil
    scr_ref[...] = val
    for r in range(dil):
        o_ref[r, :, cs] = scr_ref[pl.ds(r, n, stride=dil), :].astype(BF16)


def _inproj_kernel(x_ref, pos_ref, invf_ref, place_ref, nmix_ref, w_ref, wgg_ref, bgg_ref, qn_ref, kn_ref, bd_ref,
                   qa_ref, ka_ref, va_ref, ra_ref, la_ref, *rest):
    n_groups = len(DIL_GROUPS)
    q_refs, k_refs, v_refs = rest[:n_groups], rest[n_groups:2 * n_groups], rest[2 * n_groups:3 * n_groups]
    scratch = rest[3 * n_groups:]
    xn = _rms(x_ref[...], nmix_ref[...]).astype(BF16)

    def proj(name):
        lo, hi = PROJ_COLS[name]
        return jnp.dot(xn, w_ref[:, lo:hi], preferred_element_type=F32)

    ang = invf_ref[...] * pos_ref[...].astype(F32)
    terms = []
    for t in (jnp.cos(ang), jnp.sin(ang)):
        hi = t.astype(BF16)
        r1 = t - hi.astype(F32)
        mid = r1.astype(BF16)
        terms += [hi, mid, (r1 - mid.astype(F32)).astype(BF16)]
    tables = lax.dot_general(jnp.concatenate(terms, axis=0), place_ref[...], (((0,), (0,)), ((), ())),
                             preferred_element_type=F32)
    rot_lane = lax.broadcasted_iota(jnp.int32, (1, LANES), 1) & (HEAD_DIM - 1)
    cos = tables[:, :LANES] + jnp.where(rot_lane < ROT_DIM, 0.0, 1.0)
    sa, sb = tables[:, LANES:2 * LANES], tables[:, 2 * LANES:]
    bd = bd_ref[...]
    chunks_per_group = DIL_GW // LANES
    uses = [0]

    def emit(o_refs, j, val):
        gi = j // chunks_per_group
        scr = scratch[uses[0] % len(scratch)]
        uses[0] += 1
        _store_residue_major(o_refs[gi], (j % chunks_per_group) * LANES, val, scr, DIL_GROUPS[gi][1])

    def norm_rope(name, gain_ref, o_refs):
        y = proj(name)
        gain = gain_ref[...]
        for j in range(DIL_W // LANES):
            yc = y[:, j * LANES:(j + 1) * LANES]
            ss = jnp.dot((yc * yc).astype(BF16), bd, preferred_element_type=F32)
            yn = yc * lax.rsqrt(ss * (1.0 / HEAD_DIM) + EPS) * gain
            rot = yn * cos + pltpu.roll(yn, LANES - ROT_DIM // 2, axis=1) * sa + pltpu.roll(yn, ROT_DIM // 2, axis=1) * sb
            emit(o_refs, j, rot)

    norm_rope("qb", qn_ref, q_refs)
    norm_rope("kb", kn_ref, k_refs)
    yv = proj("vb")
    for j in range(DIL_W // LANES):
        emit(v_refs, j, yv[:, j * LANES:(j + 1) * LANES])

    za = proj("za").astype(BF16)
    gate = jnp.dot(za, wgg_ref[...], preferred_element_type=F32) + bgg_ref[...]
    la_ref[...] = jax.nn.log_sigmoid(gate) * (LOG2_E / GLA_TAU)
    ra_ref[...] = jax.nn.silu(proj("ra")).astype(BF16)
    qa_ref[...] = (proj("qa") * (GLA_DK ** -0.5)).astype(BF16)
    ka_ref[...] = proj("ka").astype(BF16)
    va_ref[...] = proj("va").astype(BF16)


def _win_kernel(w_ref, o_ref):
    gla_cols, rank = PROJ_COLS["ra"][1], GLA_GATE_RANK
    o_ref[:, :gla_cols] = w_ref[:, :gla_cols].astype(BF16)
    o_ref[:, gla_cols:PROJ_COLS["za"][0]] = w_ref[:, gla_cols + rank:].astype(BF16)
    gate = jnp.concatenate([w_ref[:, gla_cols:gla_cols + rank], jnp.zeros((w_ref.shape[0], LANES - rank), F32)], axis=1)
    o_ref[:, PROJ_COLS["za"][0]:] = gate.astype(BF16)


def _prepare_w_in(w_in, layer):
    rows = 128
    n_out = PROJ_COLS["za"][1]
    return pl.pallas_call(
        _win_kernel, grid=(D_MODEL // rows,),
        in_specs=[pl.BlockSpec((None, rows, w_in.shape[2]), lambda i: (layer, i, 0))],
        out_specs=pl.BlockSpec((rows, n_out), lambda i: (i, 0)),
        out_shape=jax.ShapeDtypeStruct((D_MODEL, n_out), BF16),
        compiler_params=_params(("parallel",)), name="prepare_w_in",
    )(w_in)


def _in_projection(x2, positions, batch, seq, norm_mix, w_all, w_gla_gate, b_gla_gate, dil_q_norm, dil_k_norm):
    tm = TM_PROJ
    ns = seq // tm
    wgg = jnp.pad(w_gla_gate.astype(BF16), ((0, LANES - GLA_GATE_RANK), (0, 0)))
    lane = jnp.arange(LANES)
    bd = (lane[:, None] // HEAD_DIM == lane[None, :] // HEAD_DIM).astype(BF16)
    qn = jnp.tile(dil_q_norm * DIL_Q_SCALE, LANES // HEAD_DIM).reshape(1, LANES)
    kn = jnp.tile(dil_k_norm, LANES // HEAD_DIM).reshape(1, LANES)
    half = ROT_DIM // 2
    invf = (ROPE_THETA ** (-jnp.arange(0, ROT_DIM, 2, dtype=F32) / ROT_DIM)).reshape(half, 1)
    rot_lane = jnp.arange(LANES) % HEAD_DIM
    freq = jnp.arange(half)[:, None]
    cos_place = ((rot_lane[None, :] % half == freq) & (rot_lane[None, :] < ROT_DIM)).astype(F32)
    sa_place = -((rot_lane[None, :] == freq)).astype(F32)
    sb_place = ((rot_lane[None, :] == freq + half)).astype(F32)
    zero = jnp.zeros_like(cos_place)
    place = jnp.concatenate([jnp.tile(jnp.concatenate([cos_place, zero, zero], axis=1), (3, 1)),
                             jnp.tile(jnp.concatenate([zero, sa_place, sb_place], axis=1), (3, 1))], axis=0).astype(BF16)
    weights = [invf, place, norm_mix.reshape(1, D_MODEL), w_all, wgg,
               b_gla_gate.reshape(1, GLA_QK), qn, kn, bd]

    def row(w):
        return pl.BlockSpec((tm, w), lambda b, s: (b * ns + s, 0))

    t = batch * seq
    flat_w = [(GLA_QK, BF16), (GLA_QK, BF16), (GLA_V, BF16), (GLA_V, BF16), (GLA_QK, F32)]
    out_specs = [row(w) for w, _ in flat_w]
    out_shape = [jax.ShapeDtypeStruct((t, w), dt) for w, dt in flat_w]
    for _ in range(3):
        for _, dil in DIL_GROUPS:
            out_specs.append(pl.BlockSpec((None, dil, tm // dil, DIL_GW), lambda b, s: (b, 0, s, 0)))
            out_shape.append(jax.ShapeDtypeStruct((batch, dil, seq // dil, DIL_GW), BF16))
    outs = pl.pallas_call(
        _inproj_kernel, grid=(batch, ns),
        in_specs=[row(D_MODEL), pl.BlockSpec((None, 1, tm), lambda b, s: (b * ns + s, 0, 0))]
        + [_resident(w.shape) for w in weights],
        out_specs=out_specs, out_shape=out_shape,
        scratch_shapes=[pltpu.VMEM((tm, LANES), F32)] * 4,
        compiler_params=_params(("parallel", "parallel")), name="in_projection",
    )(x2, positions.reshape(batch * ns, 1, tm), *weights)
    n = len(DIL_GROUPS)
    return outs[:5], outs[5:5 + n], outs[5 + n:5 + 2 * n], outs[5 + 2 * n:]


def _gla_level_refs(g):
    c = GLA_CHUNK
    refs = []
    for h in (32, 16, 8):
        parts = [jnp.broadcast_to(g[m:m + 1, :], (2 * h, g.shape[1])) for m in range(h, c, 2 * h)]
        refs.append(parts[0] if len(parts) == 1 else jnp.concatenate(parts, axis=0))
    g3 = g.reshape(c // 8, 8, g.shape[1])
    sub = lax.broadcasted_iota(jnp.int32, g3.shape, 1)

    def pick(s):
        return jnp.broadcast_to(g3[:, s:s + 1, :], g3.shape)

    refs.append(pick(4).reshape(g.shape))
    refs.append(jnp.where(sub < 4, pick(2), pick(6)).reshape(g.shape))
    r1 = jnp.where(sub < 2, pick(1), jnp.where(sub < 4, pick(3), jnp.where(sub < 6, pick(5), pick(7))))
    refs.append(r1.reshape(g.shape))
    return refs


def _gla_pair_masks():
    c = GLA_CHUNK
    ri = jnp.arange(c)[:, None]
    ci = jnp.arange(c)[None, :]
    slabs = [((ri >> (b + 1)) == (ci >> (b + 1))) & (((ri >> b) & 1) == 1) & (((ci >> b) & 1) == 0)
             for b in (5, 4, 3, 2, 1, 0)]
    slabs.append(ri == ci)
    return jnp.stack(slabs).astype(F32)


def _gla_kernel(qa_ref, ka_ref, va_ref, ra_ref, la_ref, gn_ref, pm_ref, o_ref, state_ref):
    c = GLA_CHUNK
    nb = qa_ref.shape[0]

    @pl.when(pl.program_id(1) == 0)
    def _():
        state_ref[...] = jnp.zeros_like(state_ref)

    ri = lax.broadcasted_iota(jnp.int32, (c, c), 0)
    ci = lax.broadcasted_iota(jnp.int32, (c, c), 1)
    tril = (ri >= ci).astype(BF16)
    row = lax.broadcasted_iota(jnp.int32, (c, GLA_DK), 0)
    upper = [((row >> b) & 1) == 1 for b in (5, 4, 3, 2, 1, 0)]
    gn = gn_ref[...]

    def chunk(ic, carry):
        rows = pl.ds(pl.multiple_of(ic * c, c), c)
        items = [(bb, h) for bb in range(nb) for h in range(GLA_HEADS)]
        kcol = lambda h: slice(h * GLA_DK, (h + 1) * GLA_DK)
        vcol = lambda h: slice(h * GLA_DV, (h + 1) * GLA_DV)
        nt = (((1,), (1,)), ((), ()))
        g_all = []
        for bb in range(nb):
            la = la_ref[bb, rows, :]
            la_hi = la.astype(BF16)
            la_lo = (la - la_hi.astype(F32)).astype(BF16)
            g_all.append(jnp.dot(tril, la_hi, preferred_element_type=F32)
                         + jnp.dot(tril, la_lo, preferred_element_type=F32))
        qs = [qa_ref[bb, rows, kcol(h)].astype(F32) for bb, h in items]
        ks_ = [ka_ref[bb, rows, kcol(h)].astype(F32) for bb, h in items]
        vs_ = [va_ref[bb, rows, vcol(h)] for bb, h in items]
        gs = [g_all[bb][:, kcol(h)] for bb, h in items]
        sts = [state_ref[bb, h] for bb, h in items]
        n = range(len(items))
        o_inter = [lax.dot_general((qs[i] * jnp.exp2(gs[i])).astype(BF16), sts[i].astype(BF16), nt,
                                   preferred_element_type=F32) for i in n]
        grams, diags = [], []
        for i in n:
            zs = []
            for up, gm in zip(upper, _gla_level_refs(gs[i])):
                zs.append((jnp.where(up, qs[i], ks_[i]) * jnp.exp2(-jnp.abs(gs[i] - gm))).astype(BF16))
            z = jnp.stack(zs)
            grams.append(jnp.einsum('lik,ljk->lij', z, z, preferred_element_type=F32))
            diags.append(lax.dot_general(qs[i].astype(BF16), ks_[i].astype(BF16), nt, preferred_element_type=F32))
        n_levels = len(upper)
        attn = []
        for i in n:
            a = pm_ref[n_levels] * diags[i]
            for l in range(n_levels):
                a = a + pm_ref[l] * grams[i][l]
            attn.append(a.astype(BF16))
        outs = [o_inter[i] + jnp.dot(attn[i], vs_[i], preferred_element_type=F32) for i in n]
        for i, (bb, h) in enumerate(items):
            g_last = gs[i][c - 1:c, :]
            kd = (ks_[i] * jnp.exp2(g_last - gs[i])).astype(BF16)
            state_ref[bb, h] = sts[i] * jnp.exp2(g_last) + lax.dot_general(
                vs_[i], kd, (((0,), (0,)), ((), ())), preferred_element_type=F32)
        for i, (bb, h) in enumerate(items):
            o = _rms(outs[i], gn)
            o_ref[bb, rows, vcol(h)] = (o * ra_ref[bb, rows, vcol(h)].astype(F32)).astype(BF16)
        return carry

    lax.fori_loop(0, qa_ref.shape[1] // c, chunk, 0)


def _gla(qa, ka, va, ra, la, gla_out_norm, batch, seq):
    ts = TS_GLA
    nb = GLA_SEQS
    assert batch % nb == 0 and seq % ts == 0

    def view(a):
        return a.reshape(batch, seq, a.shape[-1])

    def blk(w):
        return pl.BlockSpec((nb, ts, w), lambda b, s: (b, s, 0))

    out = pl.pallas_call(
        _gla_kernel, grid=(batch // nb, seq // ts),
        in_specs=[blk(GLA_QK), blk(GLA_QK), blk(GLA_V), blk(GLA_V), blk(GLA_QK),
                  _resident((1, GLA_DV)), _resident((7, GLA_CHUNK, GLA_CHUNK))],
        out_specs=blk(GLA_V),
        out_shape=jax.ShapeDtypeStruct((batch, seq, GLA_V), BF16),
        scratch_shapes=[pltpu.VMEM((nb, GLA_HEADS, GLA_DV, GLA_DK), F32)],
        compiler_params=_params(("parallel", "arbitrary")), name="gla",
    )(view(qa), view(ka), view(va), view(ra), view(la), gla_out_norm.reshape(1, GLA_DV), _gla_pair_masks())
    return out.reshape(batch * seq, GLA_V)


def _dil_kernel(q_ref, kp_ref, kc_ref, vp_ref, vc_ref, acc_ref, stat_ref):
    blk = DIL_BLOCK
    has_prev = pl.program_id(2) > 0
    qi = lax.broadcasted_iota(jnp.int32, (blk, 2 * blk), 0)
    kj = lax.broadcasted_iota(jnp.int32, (blk, 2 * blk), 1)
    dist = qi + blk - kj
    band = (dist >= 0) & (dist <= blk)
    band_first = band & ((kj >= blk) | has_prev)
    lane = lax.broadcasted_iota(jnp.int32, (blk, LANES), 1)
    nt = (((1,), (1,)), ((), ()))
    pairs = DIL_GW // LANES
    for res, sb in [(res, sb) for res in range(q_ref.shape[0]) for sb in range(q_ref.shape[1] // blk)]:
        rows = slice(sb * blk, (sb + 1) * blk)
        before = slice((sb - 1) * blk, sb * blk)
        mask = band_first if sb == 0 else band
        scores, values = [], []
        for p in range(pairs):
            cs = slice(p * LANES, (p + 1) * LANES)
            q2 = q_ref[res, rows, cs]
            kp, vp = ((kp_ref[res, :, cs], vp_ref[res, :, cs]) if sb == 0
                      else (kc_ref[res, before, cs], vc_ref[res, before, cs]))
            kk = jnp.concatenate([kp, kc_ref[res, rows, cs]], axis=0)
            values.append(jnp.concatenate([vp, vc_ref[res, rows, cs]], axis=0))
            for hh in range(LANES // HEAD_DIM):
                mine = (lane < HEAD_DIM) if hh == 0 else (lane >= HEAD_DIM)
                qm = jnp.where(mine, q2, jnp.zeros_like(q2))
                scores.append(lax.dot_general(qm, kk, nt, preferred_element_type=F32))
        probs = []
        for idx, s in enumerate(scores):
            s = jnp.where(mask, s, -jnp.inf)
            m = jnp.max(s, axis=-1, keepdims=True)
            e = jnp.exp2(s - m)
            l = jnp.sum(e, axis=-1, keepdims=True)
            probs.append(e.astype(BF16))
            base, half = idx * STAT_LANES, STAT_LANES // 2
            stat_ref[res, rows, base:base + half] = jnp.broadcast_to(m, (blk, half))
            stat_ref[res, rows, base + half:base + STAT_LANES] = jnp.broadcast_to(l, (blk, half))
        for p in range(pairs):
            acc0 = jnp.dot(probs[2 * p], values[p], preferred_element_type=F32)
            acc1 = jnp.dot(probs[2 * p + 1], values[p], preferred_element_type=F32)
            acc_ref[res, rows, p * LANES:(p + 1) * LANES] = jnp.where(lane < HEAD_DIM, acc0, acc1).astype(BF16)


def _dilated_group(q, k, v):
    batch, dil, length, _ = q.shape
    per_step = min(DIL_Q_BLOCKS, length // DIL_BLOCK)
    steps = length // (per_step * DIL_BLOCK)
    residues = DIL_Q_BLOCKS // per_step

    def cur(w):
        return pl.BlockSpec((None, residues, per_step * DIL_BLOCK, w), lambda b, r, n: (b, r, n, 0))

    prev = pl.BlockSpec((None, residues, DIL_BLOCK, DIL_GW), lambda b, r, n: (b, r, jnp.maximum(per_step * n - 1, 0), 0))
    stat = jax.ShapeDtypeStruct((batch, dil, length, LANES), F32)
    return pl.pallas_call(
        _dil_kernel, grid=(batch, dil // residues, steps),
        in_specs=[cur(DIL_GW), prev, cur(DIL_GW), prev, cur(DIL_GW)],
        out_specs=[cur(DIL_GW), cur(LANES)],
        out_shape=[jax.ShapeDtypeStruct(q.shape, BF16), stat],
        compiler_params=_params(("parallel", "parallel", "arbitrary")), name=f"dilated_attn_d{dil}",
    )(q, k, k, v, v)


def _load_token_major(src_ref, col, scr_ref, dil):
    cs = slice(col, col + LANES)
    if dil == 1:
        return src_ref[0, :, cs].astype(F32)
    n = src_ref.shape[1]
    for r in range(dil):
        scr_ref[pl.ds(r, n, stride=dil), :] = src_ref[r, :, cs].astype(F32)
    return scr_ref[...]


def _merge_kernel(x_ref, oa_ref, a0_ref, a1_ref, a2_ref, s0_ref, s1_ref, s2_ref, nmix_ref, wmg_ref, bmg_ref, wbg_ref,
                  wbd_ref, wmo_ref, h_ref, *scratch):
    x = x_ref[...]
    xn = _rms(x, nmix_ref[...]).astype(BF16)
    gates = jax.nn.sigmoid(jnp.dot(xn, wmg_ref[...], preferred_element_type=F32) + bmg_ref[...])
    br_a = jnp.dot(oa_ref[...], wbg_ref[...], preferred_element_type=F32)
    a_refs, s_refs = (a0_ref, a1_ref, a2_ref), (s0_ref, s1_ref, s2_ref)
    dils = [dil for _, dil in DIL_GROUPS]
    uses = [0]

    def token_major(ref, col, dil):
        uses[0] += 1
        return _load_token_major(ref, col, scratch[uses[0] % len(scratch)], dil)

    ms = [token_major(r, 0, d) for r, d in zip(s_refs, dils)]
    ls = [pltpu.roll(st, LANES - STAT_LANES // 2, axis=1) for st in ms]
    top = jnp.maximum(jnp.maximum(ms[0], ms[1]), ms[2])
    es = [jnp.exp2(m - top) for m in ms]
    inv = 1.0 / (es[0] * ls[0] + es[1] * ls[1] + es[2] * ls[2])
    def widen(w):
        cols = [jnp.broadcast_to(w[:, STAT_LANES * hd:STAT_LANES * hd + 1], (w.shape[0], HEAD_DIM))
                for hd in range(DIL_HEADS)]
        return jnp.concatenate(cols, axis=1)

    wide = [widen(e * inv) for e in es]
    mixed = []
    for j in range(DIL_GW // LANES):
        cs = slice(j * LANES, (j + 1) * LANES)
        mixed.append(sum(w[:, cs] * token_major(r, j * LANES, d) for w, r, d in zip(wide, a_refs, dils)).astype(BF16))
    br_b = jnp.dot(jnp.concatenate(mixed, axis=1), wbd_ref[...], preferred_element_type=F32)
    merged = gates[:, :D_MODEL] * br_a + gates[:, D_MODEL:] * br_b
    h_ref[...] = x + jnp.dot(merged.astype(BF16), wmo_ref[...], preferred_element_type=F32)


def _merge(x2, oa, accs, stats, batch, seq, norm_mix, w_merge_gate, b_merge_gate, w_br_gla, w_br_dil, w_mix_out):
    tm = TM_MERGE
    ns = seq // tm
    weights = [norm_mix.reshape(1, D_MODEL), w_merge_gate.astype(BF16), b_merge_gate.reshape(1, 2 * D_MODEL),
               w_br_gla.astype(BF16), w_br_dil.astype(BF16), w_mix_out.astype(BF16)]

    def row(w):
        return pl.BlockSpec((tm, w), lambda b, s: (b * ns + s, 0))

    def grouped(w):
        return [pl.BlockSpec((None, dil, tm // dil, w), lambda b, s: (b, 0, s, 0)) for _, dil in DIL_GROUPS]

    return pl.pallas_call(
        _merge_kernel, grid=(batch, ns),
        in_specs=[row(D_MODEL), row(GLA_V)] + grouped(DIL_GW) + grouped(LANES)
        + [_resident(w.shape) for w in weights],
        out_specs=row(D_MODEL), out_shape=jax.ShapeDtypeStruct((batch * seq, D_MODEL), F32),
        scratch_shapes=[pltpu.VMEM((tm, LANES), F32)] * 4,
        compiler_params=_params(("parallel", "parallel")), name="merge",
    )(x2, oa, *accs, *stats, *weights)


def _memkv_kernel(mem_ref, nmem_ref, wkv_ref, kn_ref, k_ref, v_ref):
    mn = _rms(mem_ref[...], nmem_ref[...]).astype(BF16)
    kv = jnp.dot(mn, wkv_ref[...], preferred_element_type=F32)
    kn = kn_ref[...]
    for h in range(X_HEADS):
        cs = slice(h * X_HEAD_DIM, (h + 1) * X_HEAD_DIM)
        k_ref[:, cs] = _rms(kv[:, cs], kn).astype(BF16)
    v_ref[...] = kv[:, D_MODEL:].astype(BF16)


def _mem_kv(mem, norm_mem, w_xkv, x_k_norm):
    batch, n_mem, _ = mem.shape
    blk = pl.BlockSpec((None, n_mem, D_MODEL), lambda b: (b, 0, 0))
    weights = [norm_mem.reshape(1, D_MODEL), w_xkv.astype(BF16), x_k_norm.reshape(1, X_HEAD_DIM)]
    out = jax.ShapeDtypeStruct((batch, n_mem, D_MODEL), BF16)
    return pl.pallas_call(
        _memkv_kernel, grid=(batch,),
        in_specs=[blk] + [_resident(w.shape) for w in weights],
        out_specs=[blk, blk], out_shape=[out, out],
        compiler_params=_params(("parallel",)), name="mem_kv",
    )(mem, *weights)


def _xattn_kernel(h_ref, k_ref, v_ref, nx_ref, wq_ref, qn_ref, wo_ref, o_ref):
    h = h_ref[...]
    xn = _rms(h, nx_ref[...]).astype(BF16)
    q = jnp.dot(xn, wq_ref[...], preferred_element_type=F32)
    qn = qn_ref[...]
    cols = [slice(hd * X_HEAD_DIM, (hd + 1) * X_HEAD_DIM) for hd in range(X_HEADS)]
    nt = (((1,), (1,)), ((), ()))
    scores = [lax.dot_general((_rms(q[:, cs], qn) * (X_HEAD_DIM ** -0.5 * LOG2_E)).astype(BF16), k_ref[:, cs], nt,
                              preferred_element_type=F32) for cs in cols]
    probs, inv = [], []
    for s in scores:
        e = jnp.exp2(s - jnp.max(s, axis=-1, keepdims=True))
        inv.append(1.0 / jnp.sum(e, axis=-1, keepdims=True))
        probs.append(e.astype(BF16))
    outs = [(jnp.dot(p, v_ref[:, cs], preferred_element_type=F32) * r).astype(BF16)
            for p, r, cs in zip(probs, inv, cols)]
    o = jnp.concatenate(outs, axis=-1)
    o_ref[...] = h + jnp.dot(o, wo_ref[...], preferred_element_type=F32)


def _cross_attention(h1, kmem, vmem, norm_x, w_xq, x_q_norm, w_xo, batch, seq):
    tm = TM_XATTN
    ns = seq // tm
    n_mem = kmem.shape[1]
    weights_a = [norm_x.reshape(1, D_MODEL), w_xq.astype(BF16), x_q_norm.reshape(1, X_HEAD_DIM), w_xo.astype(BF16)]
    row = pl.BlockSpec((tm, D_MODEL), lambda i: (i, 0))
    memblk = pl.BlockSpec((None, n_mem, D_MODEL), lambda i: (i // ns, 0, 0))
    return pl.pallas_call(
        _xattn_kernel, grid=(batch * ns,),
        in_specs=[row, memblk, memblk] + [_resident(w.shape) for w in weights_a],
        out_specs=row, out_shape=jax.ShapeDtypeStruct(h1.shape, F32),
        compiler_params=_params(("parallel",)), name="cross_attention",
    )(h1, kmem, vmem, *weights_a)


def _ffn_kernel(h_ref, nf_ref, wup_ref, wc_ref, bc_ref, wd_ref, o_ref, halo_ref, *stage_refs):
    tm = h_ref.shape[0]
    pad = FFN_HALO

    @pl.when(pl.program_id(1) == 0)
    def _():
        halo_ref[...] = jnp.zeros_like(halo_ref)

    h = h_ref[...]
    xn = _rms(h, nf_ref[...]).astype(BF16)
    acc = jnp.zeros((tm, D_MODEL), F32)
    for j in range(D_FF // FF_CHUNK):
        cs = slice(j * FF_CHUNK, (j + 1) * FF_CHUNK)
        stage = stage_refs[j % len(stage_refs)]
        a = jnp.dot(xn, wup_ref[:, cs], preferred_element_type=F32)
        u = jnp.dot(xn, wup_ref[:, D_FF + j * FF_CHUNK:D_FF + (j + 1) * FF_CHUNK], preferred_element_type=F32)
        stage[0:pad, :] = halo_ref[:, cs]
        stage[pad:pad + tm, :] = a
        halo_ref[:, cs] = a[tm - pad:tm, :]
        conv = bc_ref[:, cs] + a * wc_ref[2:3, cs]
        conv = conv + stage[pad - 1:pad - 1 + tm, :] * wc_ref[1:2, cs]
        conv = conv + stage[pad - 2:pad - 2 + tm, :] * wc_ref[0:1, cs]
        gelu = 0.5 * conv * (1.0 + lax.erf(conv * (2.0 ** -0.5)))
        acc = acc + jnp.dot((gelu * u).astype(BF16), wd_ref[cs, :], preferred_element_type=F32)
    o_ref[...] = h + acc


def _ffn(h2, norm_ffn, w_ffn_up, w_ffn_conv, b_ffn_conv, w_ffn_down, batch, seq):
    tm = TM_FFN
    ns = seq // tm
    weights = [norm_ffn.reshape(1, D_MODEL), w_ffn_up.astype(BF16),
               w_ffn_conv, b_ffn_conv.reshape(1, D_FF), w_ffn_down.astype(BF16)]
    row = pl.BlockSpec((tm, D_MODEL), lambda b, s: (b * ns + s, 0))
    return pl.pallas_call(
        _ffn_kernel, grid=(batch, ns),
        in_specs=[row] + [_resident(w.shape) for w in weights],
        out_specs=row, out_shape=jax.ShapeDtypeStruct(h2.shape, F32),
        scratch_shapes=[pltpu.VMEM((FFN_HALO, D_FF), F32)]
        + [pltpu.VMEM((tm + FFN_HALO, FF_CHUNK), F32)] * min(2, D_FF // FF_CHUNK),
        compiler_params=_params(("parallel", "arbitrary")), name="conv_glu_ffn",
    )(h2, *weights)


def _layer(h, positions, mem, batch, seq, w_all, norm_mix, w_gla_gate, b_gla_gate, gla_out_norm, dil_q_norm, dil_k_norm,
           w_br_gla, w_br_dil, w_merge_gate, b_merge_gate, w_mix_out, norm_x, norm_mem, w_xq, w_xkv, x_q_norm,
           x_k_norm, w_xo, norm_ffn, w_ffn_up, w_ffn_conv, b_ffn_conv, w_ffn_down):
    (qa, ka, va, ra, la), qs, ks, vs = _in_projection(h, positions, batch, seq, norm_mix, w_all, w_gla_gate,
                                                      b_gla_gate, dil_q_norm, dil_k_norm)
    oa = _gla(qa, ka, va, ra, la, gla_out_norm, batch, seq)
    accs, stats = zip(*(_dilated_group(q, k, v) for q, k, v in zip(qs, ks, vs)))
    h1 = _merge(h, oa, accs, stats, batch, seq, norm_mix, w_merge_gate, b_merge_gate, w_br_gla, w_br_dil, w_mix_out)
    kmem, vmem = _mem_kv(mem, norm_mem, w_xkv, x_k_norm)
    h2 = _cross_attention(h1, kmem, vmem, norm_x, w_xq, x_q_norm, w_xo, batch, seq)
    return _ffn(h2, norm_ffn, w_ffn_up, w_ffn_conv, b_ffn_conv, w_ffn_down, batch, seq)


def kernel(x, mem, positions, norm_mix, w_in, w_gla_gate, b_gla_gate, gla_out_norm, dil_q_norm, dil_k_norm, w_br_gla, w_br_dil, w_merge_gate, b_merge_gate, w_mix_out, norm_x, norm_mem, w_xq, w_xkv, x_q_norm, x_k_norm, w_xo, norm_ffn, w_ffn_up, w_ffn_conv, b_ffn_conv, w_ffn_down):
    batch, seq, _ = x.shape
    h = x.reshape(batch * seq, D_MODEL)
    stacked = (norm_mix, w_gla_gate, b_gla_gate, gla_out_norm, dil_q_norm, dil_k_norm, w_br_gla, w_br_dil,
               w_merge_gate, b_merge_gate, w_mix_out, norm_x, norm_mem, w_xq, w_xkv, x_q_norm, x_k_norm, w_xo,
               norm_ffn, w_ffn_up, w_ffn_conv, b_ffn_conv, w_ffn_down)
    for l in range(norm_mix.shape[0]):
        h = _layer(h, positions, mem, batch, seq, _prepare_w_in(w_in, l), *(p[l] for p in stacked))
    return h.reshape(batch, seq, D_MODEL)
```
